```python
import jax, jax.numpy as jnp
from jax import lax
import numpy as np

D_MODEL = 1024
BATCH = 16
SEQ = 4096
DEPTH = 1

EPS = 1e-6
D_MIX = D_MODEL
D_RNN = D_MIX // 2
RNN_BLOCKS = 8
RNN_BW = D_RNN // RNN_BLOCKS
CONV_W = 4
LRU_C = 8.0
MLA_HEADS = 8
QK_NOPE = 64
QK_ROPE = 32
V_DIM = 64
Q_LORA = 256
KV_LORA = 128
D_ATT = MLA_HEADS * V_DIM
QK_DIM = QK_NOPE + QK_ROPE
ATT_SCALE = QK_DIM ** -0.5
ROPE_BASE = 10000.0
Q_BLOCK = 128
D_IN = 2 * D_RNN + Q_LORA + KV_LORA + QK_ROPE
IN_SPLITS = (D_RNN, 2 * D_RNN, 2 * D_RNN + Q_LORA, 2 * D_RNN + Q_LORA + KV_LORA)
D_FF = ((8 * D_MODEL // 3 + 255) // 256) * 256
PLE_DIM = 256
MAX_POS_OFFSET = 1024

kernel_name = 'hymba_hawk_mla_hybrid'


def rmsnorm(x, g):
    xf = x.astype(jnp.float32)
    y = xf * lax.rsqrt(jnp.mean(xf * xf, axis=-1, keepdims=True) + EPS)
    return (y * g.astype(jnp.float32)).astype(x.dtype)


def rope_cos_sin(positions, dim):
    inv = ROPE_BASE ** (-jnp.arange(0, dim, 2, dtype=jnp.float32) / dim)
    ang = positions.astype(jnp.float32)[..., None] * inv
    return jnp.cos(ang), jnp.sin(ang)


def apply_rope(t, cos, sin):
    t1, t2 = jnp.split(t.astype(jnp.float32), 2, axis=-1)
    out = jnp.concatenate([t1 * cos - t2 * sin, t1 * sin + t2 * cos], axis=-1)
    return out.astype(t.dtype)


def causal_depthwise_conv(x, w, b):
    c = x.shape[-1]
    y = lax.conv_general_dilated(
        x, w[:, None, :].astype(x.dtype), window_strides=(1,),
        padding=[(CONV_W - 1, 0)], dimension_numbers=('NWC', 'WIO', 'NWC'),
        feature_group_count=c)
    return y + b.astype(x.dtype)


def _lin_rec_combine(left, right):
    a1, b1 = left
    a2, b2 = right
    return a1 * a2, a2 * b1 + b2


def rglru_group(x_rnn, x_gate, conv_w, conv_b, w_a, b_a, w_x, b_x, lru_L):
    B, S, _ = x_rnn.shape
    xc = causal_depthwise_conv(x_rnn, conv_w, conv_b)
    xb = xc.reshape(B, S, RNN_BLOCKS, RNN_BW)
    r = jax.nn.sigmoid(jnp.einsum('bsnc,ncd->bsnd', xb, w_a) + b_a).reshape(B, S, D_RNN)
    i = jax.nn.sigmoid(jnp.einsum('bsnc,ncd->bsnd', xb, w_x) + b_x).reshape(B, S, D_RNN)
    log_a = -LRU_C * r.astype(jnp.float32) * jax.nn.softplus(-lru_L.astype(jnp.float32))
    a = jnp.exp(log_a)
    mult = jnp.sqrt(-jnp.expm1(2.0 * log_a))
    bterm = mult * (i * xc).astype(jnp.float32)
    _, h = lax.associative_scan(_lin_rec_combine, (a, bterm), axis=1)
    return h.astype(x_rnn.dtype) * jax.nn.gelu(x_gate)


def causal_mla_attention(q_nope, q_rope, k_nope, k_rope, v):
    B, S, H, _ = q_nope.shape
    nb = S // Q_BLOCK
    k_idx = jnp.arange(S)

    def to_blocks(t):
        return jnp.moveaxis(t.reshape(B, nb, Q_BLOCK, *t.shape[2:]), 1, 0)

    def one_block(args):
        qn, qr, blk = args
        s = jnp.einsum('bqhd,bkhd->bhqk', qn, k_nope, preferred_element_type=jnp.float32)
        s = s + jnp.einsum('bqhr,bkr->bhqk', qr, k_rope, preferred_element_type=jnp.float32)
        q_idx = blk * Q_BLOCK + jnp.arange(Q_BLOCK)
        mask = k_idx[None, :] <= q_idx[:, None]
        s = jnp.where(mask[None, None], s * ATT_SCALE, -jnp.inf)
        pr = jax.nn.softmax(s, axis=-1).astype(v.dtype)
        return jnp.einsum('bhqk,bkhd->bqhd', pr, v)

    out = lax.map(one_block, (to_blocks(q_nope), to_blocks(q_rope), jnp.arange(nb)))
    return jnp.moveaxis(out, 0, 1).reshape(B, S, H * V_DIM)


def setup_inputs(seed: int = 0) -> dict:
    key = jax.random.key(seed)
    ks = jax.random.split(key, 32)
    f32 = jnp.float32

    def nrm(k, shape, fan_in):
        return jax.random.normal(k, shape, f32) * fan_in ** -0.5

    def gain(k, shape):
        return 1.0 + 0.02 * jax.random.normal(k, shape, f32)

    def small(k, shape):
        return 0.01 * jax.random.normal(k, shape, f32)

    L = DEPTH
    x = jax.random.normal(ks[0], (BATCH, SEQ, D_MODEL), f32)
    p = jax.random.normal(ks[1], (DEPTH, BATCH, SEQ, PLE_DIM), f32)
    offs = jax.random.randint(ks[2], (BATCH, 1), 0, MAX_POS_OFFSET, dtype=jnp.int32)
    positions = (offs + jnp.arange(SEQ, dtype=jnp.int32)[None, :]).astype(jnp.int32)
    u = jax.random.uniform(ks[3], (L, D_RNN), f32, minval=0.9, maxval=0.999)
    a0 = u ** (1.0 / LRU_C)
    lru_L = jnp.log(a0) - jnp.log1p(-a0)
    return {
        'x': x,
        'p': p,
        'positions': positions,
        'g_mix': gain(ks[4], (L, D_MODEL)),
        'w_in': nrm(ks[5], (L, D_MODEL, D_IN), D_MODEL),
        'conv_w': nrm(ks[6], (L, CONV_W, D_RNN), CONV_W),
        'conv_b': small(ks[7], (L, D_RNN)),
        'w_rg_a': nrm(ks[8], (L, RNN_BLOCKS, RNN_BW, RNN_BW), RNN_BW),
        'b_rg_a': small(ks[9], (L, RNN_BLOCKS, RNN_BW)),
        'w_rg_x': nrm(ks[10], (L, RNN_BLOCKS, RNN_BW, RNN_BW), RNN_BW),
        'b_rg_x': small(ks[11], (L, RNN_BLOCKS, RNN_BW)),
        'lru_L': lru_L,
        'g_q_lat': gain(ks[12], (L, Q_LORA)),
        'w_q_up': nrm(ks[13], (L, Q_LORA, MLA_HEADS * QK_DIM), Q_LORA),
        'g_kv_lat': gain(ks[14], (L, KV_LORA)),
        'w_kv_up': nrm(ks[15], (L, KV_LORA, MLA_HEADS * (QK_NOPE + V_DIM)), KV_LORA),
        'g_out_rnn': gain(ks[16], (L, D_RNN)),
        'g_out_att': gain(ks[17], (L, D_ATT)),
        'w_out': nrm(ks[18], (L, D_MIX, D_MODEL), D_MIX),
        'g_ffn': gain(ks[19], (L, D_MODEL)),
        'w_ffn_gate': nrm(ks[20], (L, D_MODEL, D_FF), D_MODEL),
        'w_ffn_up': nrm(ks[21], (L, D_MODEL, D_FF), D_MODEL),
        'w_ffn_down': nrm(ks[22], (L, D_FF, D_MODEL), D_FF),
        'g_ple_in': gain(ks[23], (L, D_MODEL)),
        'w_ple_gate': nrm(ks[24], (L, D_MODEL, D_MODEL), D_MODEL),
        'w_ple_proj': nrm(ks[25], (L, PLE_DIM, D_MODEL), PLE_DIM),
        'g_ple_post': gain(ks[26], (L, D_MODEL)),
        'g_final': gain(ks[27], (D_MODEL,)),
    }


def reference(x, p, positions, g_mix, w_in, conv_w, conv_b, w_rg_a, b_rg_a, w_rg_x,
              b_rg_x, lru_L, g_q_lat, w_q_up, g_kv_lat, w_kv_up, g_out_rnn, g_out_att,
              w_out, g_ffn, w_ffn_gate, w_ffn_up, w_ffn_down, g_ple_in, w_ple_gate,
              w_ple_proj, g_ple_post, g_final):
    B, S, _ = x.shape
    cos, sin = rope_cos_sin(positions, QK_ROPE)
    h = x
    for l in range(DEPTH):
        u = rmsnorm(h, g_mix[l])
        z = u @ w_in[l]
        x_rnn, x_gate, c_q, c_kv, k_rope = jnp.split(z, IN_SPLITS, axis=-1)

        y_rnn = rglru_group(x_rnn, x_gate, conv_w[l], conv_b[l], w_rg_a[l], b_rg_a[l],
                            w_rg_x[l], b_rg_x[l], lru_L[l])

        q = (rmsnorm(c_q, g_q_lat[l]) @ w_q_up[l]).reshape(B, S, MLA_HEADS, QK_DIM)
        q_nope, q_rope = jnp.split(q, [QK_NOPE], axis=-1)
        q_rope = apply_rope(q_rope, cos[:, :, None, :], sin[:, :, None, :])
        kv = (rmsnorm(c_kv, g_kv_lat[l]) @ w_kv_up[l]).reshape(B, S, MLA_HEADS, QK_NOPE + V_DIM)
        k_nope, v = jnp.split(kv, [QK_NOPE], axis=-1)
        k_rope = apply_rope(k_rope, cos, sin)
        y_att = causal_mla_attention(q_nope, q_rope, k_nope, k_rope, v)

        y = jnp.concatenate([rmsnorm(y_rnn, g_out_rnn[l]), rmsnorm(y_att, g_out_att[l])], axis=-1)
        h = h + y @ w_out[l]

        vff = rmsnorm(h, g_ffn[l])
        h = h + (jax.nn.silu(vff @ w_ffn_gate[l]) * (vff @ w_ffn_up[l])) @ w_ffn_down[l]

        e = rmsnorm(p[l] @ w_ple_proj[l], g_ple_post[l])
        gate = jax.nn.sigmoid(rmsnorm(h, g_ple_in[l]) @ w_ple_gate[l])
        h = h + gate * e
    return rmsnorm(h, g_final)
```

```python
import functools

import jax
import jax.numpy as jnp
from jax import lax
from jax.experimental import pallas as pl
from jax.experimental.pallas import tpu as pltpu

F32 = jnp.float32
BF16 = jnp.bfloat16

EPS = 1e-6
D_MODEL = 1024
D_RNN = 512
RNN_BLOCKS = 8
RNN_BW = 64
CONV_W = 4
LRU_C = 8.0
MLA_HEADS = 8
QK_NOPE = 64
QK_ROPE = 32
V_DIM = 64
Q_LORA = 256
KV_LORA = 128
D_ATT = MLA_HEADS * V_DIM
QK_DIM = QK_NOPE + QK_ROPE
ATT_SCALE = QK_DIM ** -0.5
ROPE_BASE = 10000.0
D_FF = 2816
PLE_DIM = 256

LANES = 128
SUBLANES = 8
HEAD_PAD = LANES
D_IN_EXT = 1536
FF_CHUNK = 256
VMEM_LIMIT = 56 * 1024 * 1024


def _rms(x, g):
    ms = jnp.mean(x * x, axis=-1, keepdims=True)
    return x * lax.rsqrt(ms + EPS) * g


def _dot(a, b):
    return jnp.dot(a, b, preferred_element_type=F32)


def _dot_nt(a, b):
    return lax.dot_general(a, b, (((1,), (1,)), ((), ())), preferred_element_type=F32)


def _proj_kernel(x_ref, tabk_ref, tabq_ref, gmix_ref, win_ref, gq_ref, wqt_ref, gkv_ref,
                 wk_ref, wvt_ref, zr_ref, qt_ref, k_ref, vt_ref, *, att_t):
    u = _rms(x_ref[...], gmix_ref[...]).astype(BF16)
    z = _dot(u, win_ref[...])
    zr_ref[...] = z[:, :2 * D_RNN].astype(BF16)
    c_q = z[:, 2 * D_RNN:2 * D_RNN + Q_LORA]
    c_kv = z[:, 2 * D_RNN + Q_LORA:2 * D_RNN + Q_LORA + KV_LORA]
    zk = z[:, 2 * D_RNN + Q_LORA + KV_LORA:]

    cqn = _rms(c_q, gq_ref[...]).astype(BF16)
    qt = _dot_nt(wqt_ref[...], cqn)
    half = QK_ROPE // 2
    cos_t = tabq_ref[0:half, :]
    sin_t = tabq_ref[half:QK_ROPE, :]
    for h in range(MLA_HEADS):
        base = h * HEAD_PAD
        r0 = base + QK_NOPE
        qt_ref[base:r0, :] = (qt[base:r0] * ATT_SCALE).astype(BF16)
        t1 = qt[r0:r0 + half]
        t2 = qt[r0 + half:r0 + QK_ROPE]
        qt_ref[r0:r0 + half, :] = ((t1 * cos_t - t2 * sin_t) * ATT_SCALE).astype(BF16)
        qt_ref[r0 + half:r0 + QK_ROPE, :] = ((t1 * sin_t + t2 * cos_t) * ATT_SCALE).astype(BF16)
        qt_ref[r0 + QK_ROPE:base + HEAD_PAD, :] = jnp.zeros(
            (HEAD_PAD - QK_DIM, qt.shape[1]), BF16)

    ckvn = _rms(c_kv, gkv_ref[...]).astype(BF16)
    prod = zk * tabk_ref[...]
    kr = prod + pltpu.roll(prod, LANES - QK_ROPE, 1)
    kcat = jnp.concatenate([ckvn, kr.astype(BF16)], axis=1)
    k_ref[...] = _dot(kcat, wk_ref[...]).astype(BF16)
    vt = _dot_nt(wvt_ref[...], ckvn).astype(BF16)
    for c in range(vt.shape[1] // att_t):
        vt_ref[c] = vt[:, c * att_t:(c + 1) * att_t]


def _rglru_kernel(zr_ref, convw_ref, convb_ref, wg_ref, ba_ref, bx_ref, lru_ref, gout_ref,
                  y_ref, xpad_ref, a_ref, b_ref, carry_ref):
    t = zr_ref.shape[0]
    halo = SUBLANES

    @pl.when(pl.program_id(1) == 0)
    def _():
        xpad_ref[0:halo, :] = jnp.zeros((halo, D_RNN), F32)
        carry_ref[...] = jnp.zeros_like(carry_ref)

    xpad_ref[halo:, :] = zr_ref[:, :D_RNN].astype(F32)
    w = convw_ref[...]
    xc = convb_ref[...] + w[CONV_W - 1:CONV_W] * xpad_ref[halo:halo + t, :]
    for j in range(1, CONV_W):
        xc = xc + w[CONV_W - 1 - j:CONV_W - j] * xpad_ref[halo - j:halo - j + t, :]
    xpad_ref[0:halo, :] = xpad_ref[t:t + halo, :]

    neg_l = -lru_ref[...]
    sp = jnp.maximum(neg_l, 0.0) + jnp.log1p(jnp.exp(-jnp.abs(neg_l)))
    xcb = xc.astype(BF16)
    for g in range(D_RNN // LANES):
        sl = slice(g * LANES, (g + 1) * LANES)
        pre = _dot(xcb[:, sl], wg_ref[g])
        r = jax.nn.sigmoid(pre[:, :LANES] + ba_ref[:, sl])
        i = jax.nn.sigmoid(pre[:, LANES:] + bx_ref[:, sl])
        log_a = (-LRU_C * r) * sp[:, sl]
        a_ref[:, sl] = jnp.exp(log_a)
        th = jnp.tanh(log_a)
        b_ref[:, sl] = jnp.sqrt(-2.0 * th / (1.0 - th)) * (i * xc[:, sl])

    sub = lax.broadcasted_iota(jnp.int32, (SUBLANES, D_RNN), 0)

    def group(gi, carry):
        rows = pl.ds(pl.multiple_of(gi * SUBLANES, SUBLANES), SUBLANES)
        ag = a_ref[rows, :]
        bg = b_ref[rows, :]
        for d in (1, 2, 4):
            ash = jnp.where(sub >= d, pltpu.roll(ag, d, 0), 1.0)
            bsh = jnp.where(sub >= d, pltpu.roll(bg, d, 0), 0.0)
            bg = ag * bsh + bg
            ag = ag * ash
        hg = bg + ag * carry
        b_ref[rows, :] = hg
        return hg[SUBLANES - 1:SUBLANES, :]

    carry_ref[...] = lax.fori_loop(0, t // SUBLANES, group, carry_ref[...], unroll=4)

    y = b_ref[...] * jax.nn.gelu(zr_ref[:, D_RNN:].astype(F32))
    y_ref[...] = _rms(y, gout_ref[...]).astype(BF16)


def _attn_kernel(qt_ref, k_ref, vt_ref, o_ref, *, att_t):
    s_len = k_ref.shape[0]
    t = att_t
    kidx = lax.broadcasted_iota(jnp.int32, (t, t), 0)
    qidx = lax.broadcasted_iota(jnp.int32, (t, t), 1)
    causal = kidx <= qidx

    for qi in range(s_len // t):
        qt = qt_ref[:, qi * t:(qi + 1) * t]
        s = _dot(k_ref[qi * t:(qi + 1) * t, :], qt)
        s = jnp.where(causal, s, -jnp.inf)
        m = jnp.max(s, axis=0, keepdims=True)
        p = jnp.exp(s - m)
        l = jnp.sum(p, axis=0, keepdims=True)
        acc = _dot(vt_ref[qi], p.astype(BF16))

        def body(j, carry, qt=qt):
            m, l, acc = carry
            kj = k_ref[pl.ds(pl.multiple_of(j * t, t), t), :]
            s = _dot(kj, qt)
            m_new = jnp.maximum(m, jnp.max(s, axis=0, keepdims=True))
            p = jnp.exp(s - m_new)
            alpha = jnp.exp(m - m_new)
            l = alpha * l + jnp.sum(p, axis=0, keepdims=True)
            acc = alpha * acc + _dot(vt_ref[j], p.astype(BF16))
            return m_new, l, acc

        m, l, acc = lax.fori_loop(0, qi, body, (m, l, acc))
        o_ref[:, qi * t:(qi + 1) * t] = acc / l


def _mlp_kernel(x_ref, yr_ref, ot_ref, p_ref, gatt_ref, wo1_ref, wo2_ref, gffn_ref, wgu_ref,
                wd_ref, gpin_ref, wpg_ref, wpp_ref, gpost_ref, gfin_ref, out_ref,
                acc_ref, vff_ref):
    ya = _rms(ot_ref[...].T, gatt_ref[...]).astype(BF16)
    h = x_ref[...] + _dot(yr_ref[...], wo1_ref[...]) + _dot(ya, wo2_ref[...])
    acc_ref[...] = h
    vff_ref[...] = _rms(h, gffn_ref[...]).astype(BF16)

    def chunk(c, carry):
        gu = _dot(vff_ref[...], wgu_ref[c])
        act = (jax.nn.silu(gu[:, :FF_CHUNK]) * gu[:, FF_CHUNK:]).astype(BF16)
        acc_ref[...] += _dot(act, wd_ref[c])
        return carry

    lax.fori_loop(0, wgu_ref.shape[0], chunk, 0)
    h = acc_ref[...]
    e = _rms(_dot(p_ref[...].astype(BF16), wpp_ref[...]), gpost_ref[...])
    gate = jax.nn.sigmoid(_dot(_rms(h, gpin_ref[...]).astype(BF16), wpg_ref[...]))
    out_ref[...] = _rms(h + gate * e, gfin_ref[...])


def _const_spec(shape):
    zeros = (0,) * len(shape)
    return pl.BlockSpec(shape, lambda *_: zeros)


def _tiles(seq):
    proj_t = min(512, seq)
    att_t = min(256, seq)
    return proj_t, min(512, seq), att_t, min(512, seq)


def _layer(h, p_l, tabk, tabq, prm):
    b, s, _ = h.shape
    proj_t, rnn_t, att_t, mlp_t = _tiles(s)
    n = b * s
    cparams = functools.partial(pltpu.CompilerParams, vmem_limit_bytes=VMEM_LIMIT)

    zr, qt, k, vt = pl.pallas_call(
        functools.partial(_proj_kernel, att_t=att_t),
        grid=(b, s // proj_t),
        in_specs=[
            pl.BlockSpec((None, proj_t, D_MODEL), lambda i, j: (i, j, 0)),
            pl.BlockSpec((None, proj_t, LANES), lambda i, j: (i, j, 0)),
            pl.BlockSpec((None, QK_ROPE, proj_t), lambda i, j: (i, 0, j)),
            _const_spec((1, D_MODEL)),
            _const_spec((D_MODEL, D_IN_EXT)),
            _const_spec((1, Q_LORA)),
            _const_spec((MLA_HEADS * HEAD_PAD, Q_LORA)),
            _const_spec((1, KV_LORA)),
            _const_spec((2 * LANES, MLA_HEADS * HEAD_PAD)),
            _const_spec((D_ATT, KV_LORA)),
        ],
        out_specs=[
            pl.BlockSpec((None, proj_t, 2 * D_RNN), lambda i, j: (i, j, 0)),
            pl.BlockSpec((None, MLA_HEADS * HEAD_PAD, proj_t), lambda i, j: (i, 0, j)),
            pl.BlockSpec((None, proj_t, MLA_HEADS * HEAD_PAD), lambda i, j: (i, j, 0)),
            pl.BlockSpec((None, proj_t // att_t, D_ATT, att_t), lambda i, j: (i, j, 0, 0)),
        ],
        out_shape=[
            jax.ShapeDtypeStruct((b, s, 2 * D_RNN), BF16),
            jax.ShapeDtypeStruct((b, MLA_HEADS * HEAD_PAD, s), BF16),
            jax.ShapeDtypeStruct((b, s, MLA_HEADS * HEAD_PAD), BF16),
            jax.ShapeDtypeStruct((b, s // att_t, D_ATT, att_t), BF16),
        ],
        compiler_params=cparams(dimension_semantics=("arbitrary", "arbitrary")),
        name="proj",
    )(h, tabk, tabq, prm["g_mix"], prm["w_in"], prm["g_q"], prm["wq_t"], prm["g_kv"],
      prm["w_k"], prm["wv_t"])

    y_rnn = pl.pallas_call(
        _rglru_kernel,
        grid=(b, s // rnn_t),
        in_specs=[
            pl.BlockSpec((None, rnn_t, 2 * D_RNN), lambda i, j: (i, j, 0)),
            _const_spec((CONV_W, D_RNN)),
            _const_spec((1, D_RNN)),
            _const_spec((D_RNN // LANES, LANES, 2 * LANES)),
            _const_spec((1, D_RNN)),
            _const_spec((1, D_RNN)),
            _const_spec((1, D_RNN)),
            _const_spec((1, D_RNN)),
        ],
        out_specs=pl.BlockSpec((None, rnn_t, D_RNN), lambda i, j: (i, j, 0)),
        out_shape=jax.ShapeDtypeStruct((b, s, D_RNN), BF16),
        scratch_shapes=[
            pltpu.VMEM((rnn_t + SUBLANES, D_RNN), F32),
            pltpu.VMEM((rnn_t, D_RNN), F32),
            pltpu.VMEM((rnn_t, D_RNN), F32),
            pltpu.VMEM((1, D_RNN), F32),
        ],
        compiler_params=cparams(dimension_semantics=("arbitrary", "arbitrary")),
        name="rglru",
    )(zr, prm["conv_w"], prm["conv_b"], prm["w_gates"], prm["b_a"], prm["b_x"], prm["lru_l"],
      prm["g_out_rnn"])

    o_t = pl.pallas_call(
        functools.partial(_attn_kernel, att_t=att_t),
        grid=(b, MLA_HEADS),
        in_specs=[
            pl.BlockSpec((None, HEAD_PAD, s), lambda i, j: (i, j, 0)),
            pl.BlockSpec((None, s, HEAD_PAD), lambda i, j: (i, 0, j)),
            pl.BlockSpec((None, s // att_t, V_DIM, att_t), lambda i, j: (i, 0, j, 0)),
        ],
        out_specs=pl.BlockSpec((None, V_DIM, s), lambda i, j: (i, j, 0)),
        out_shape=jax.ShapeDtypeStruct((b, D_ATT, s), F32),
        compiler_params=cparams(dimension_semantics=("arbitrary", "arbitrary")),
        name="attn",
    )(qt, k, vt)

    spt = s // mlp_t
    out = pl.pallas_call(
        _mlp_kernel,
        grid=(n // mlp_t,),
        in_specs=[
            pl.BlockSpec((mlp_t, D_MODEL), lambda i: (i, 0)),
            pl.BlockSpec((mlp_t, D_RNN), lambda i: (i, 0)),
            pl.BlockSpec((None, D_ATT, mlp_t), lambda i: (i // spt, 0, i % spt)),
            pl.BlockSpec((mlp_t, PLE_DIM), lambda i: (i, 0)),
            _const_spec((1, D_ATT)),
            _const_spec((D_RNN, D_MODEL)),
            _const_spec((D_ATT, D_MODEL)),
            _const_spec((1, D_MODEL)),
            _const_spec((D_FF // FF_CHUNK, D_MODEL, 2 * FF_CHUNK)),
            _const_spec((D_FF // FF_CHUNK, FF_CHUNK, D_MODEL)),
            _const_spec((1, D_MODEL)),
            _const_spec((D_MODEL, D_MODEL)),
            _const_spec((PLE_DIM, D_MODEL)),
            _const_spec((1, D_MODEL)),
            _const_spec((1, D_MODEL)),
        ],
        out_specs=pl.BlockSpec((mlp_t, D_MODEL), lambda i: (i, 0)),
        out_shape=jax.ShapeDtypeStruct((n, D_MODEL), F32),
        scratch_shapes=[
            pltpu.VMEM((mlp_t, D_MODEL), F32),
            pltpu.VMEM((mlp_t, D_MODEL), BF16),
        ],
        compiler_params=cparams(dimension_semantics=("arbitrary",)),
        name="mlp",
    )(h.reshape(n, D_MODEL), y_rnn.reshape(n, D_RNN), o_t, p_l.reshape(n, PLE_DIM),
      prm["g_out_att"], prm["w_out_rnn"], prm["w_out_att"], prm["g_ffn"], prm["w_gu"],
      prm["w_down"], prm["g_ple_in"], prm["w_ple_gate"], prm["w_ple_proj"], prm["g_ple_post"],
      prm["g_final"])
    return out.reshape(b, s, D_MODEL)


def _prep_layer(l, g_mix, w_in, conv_w, conv_b, w_rg_a, b_rg_a, w_rg_x, b_rg_x, lru_L, g_q_lat,
                w_q_up, g_kv_lat, w_kv_up, g_out_rnn, g_out_att, w_out, g_ffn, w_ffn_gate,
                w_ffn_up, w_ffn_down, g_ple_in, w_ple_gate, w_ple_proj, g_ple_post, g_final):
    row = lambda v: v.reshape(1, -1).astype(F32)
    half = QK_ROPE // 2
    w = w_in[l]
    w_kr = w[:, -QK_ROPE:]
    w_kr_rot = jnp.concatenate([-w_kr[:, half:], w_kr[:, :half]], axis=1)
    w_in_ext = jnp.concatenate(
        [w, w_kr_rot, jnp.zeros((D_MODEL, D_IN_EXT - w.shape[1] - QK_ROPE), F32)], axis=1)

    wq = w_q_up[l].reshape(Q_LORA, MLA_HEADS, QK_DIM)
    wq = jnp.pad(wq, ((0, 0), (0, 0), (0, HEAD_PAD - QK_DIM)))
    wq_t = wq.reshape(Q_LORA, MLA_HEADS * HEAD_PAD).T

    wkv = w_kv_up[l].reshape(KV_LORA, MLA_HEADS, QK_NOPE + V_DIM)
    wk_nope = jnp.pad(wkv[:, :, :QK_NOPE], ((0, 0), (0, 0), (0, HEAD_PAD - QK_NOPE)))
    wk_nope = wk_nope.reshape(KV_LORA, MLA_HEADS * HEAD_PAD)
    place = jnp.pad(jnp.eye(QK_ROPE, dtype=F32), ((0, LANES - QK_ROPE), (QK_NOPE, HEAD_PAD - QK_DIM)))
    w_k = jnp.concatenate([wk_nope, jnp.tile(place, (1, MLA_HEADS))], axis=0)
    wv_t = wkv[:, :, QK_NOPE:].reshape(KV_LORA, D_ATT).T

    def blockdiag(wb):
        wb = wb.reshape(D_RNN // LANES, 2, RNN_BW, RNN_BW)
        z = jnp.zeros_like(wb[:, 0])
        top = jnp.concatenate([wb[:, 0], z], axis=2)
        bot = jnp.concatenate([z, wb[:, 1]], axis=2)
        return jnp.concatenate([top, bot], axis=1)

    w_gates = jnp.concatenate([blockdiag(w_rg_a[l]), blockdiag(w_rg_x[l])], axis=2)

    nc = D_FF // FF_CHUNK
    wg = w_ffn_gate[l].reshape(D_MODEL, nc, FF_CHUNK)
    wu = w_ffn_up[l].reshape(D_MODEL, nc, FF_CHUNK)
    w_gu = jnp.concatenate([wg, wu], axis=2).transpose(1, 0, 2)

    return dict(
        g_mix=row(g_mix[l]), w_in=w_in_ext.astype(BF16), g_q=row(g_q_lat[l]),
        wq_t=wq_t.astype(BF16), g_kv=row(g_kv_lat[l]), w_k=w_k.astype(BF16),
        wv_t=wv_t.astype(BF16),
        conv_w=conv_w[l].astype(F32), conv_b=row(conv_b[l]), w_gates=w_gates.astype(BF16),
        b_a=row(b_rg_a[l]), b_x=row(b_rg_x[l]), lru_l=row(lru_L[l]), g_out_rnn=row(g_out_rnn[l]),
        g_out_att=row(g_out_att[l]), w_out_rnn=w_out[l][:D_RNN].astype(BF16),
        w_out_att=w_out[l][D_RNN:].astype(BF16), g_ffn=row(g_ffn[l]), w_gu=w_gu.astype(BF16),
        w_down=w_ffn_down[l].reshape(nc, FF_CHUNK, D_MODEL).astype(BF16),
        g_ple_in=row(g_ple_in[l]), w_ple_gate=w_ple_gate[l].astype(BF16),
        w_ple_proj=w_ple_proj[l].astype(BF16), g_ple_post=row(g_ple_post[l]),
        g_final=row(g_final),
    )


def _rope_tables(positions):
    half = QK_ROPE // 2
    inv = ROPE_BASE ** (-jnp.arange(0, QK_ROPE, 2, dtype=F32) / QK_ROPE)
    ang = positions.astype(F32)[..., None] * inv
    cos, sin = jnp.cos(ang), jnp.sin(ang)
    pad = jnp.zeros(ang.shape[:-1] + (LANES - 2 * QK_ROPE,), F32)
    tabk = jnp.concatenate([cos, cos, sin, sin, pad], axis=-1)
    tabq = jnp.concatenate([cos, sin], axis=-1).transpose(0, 2, 1)
    del half
    return tabk, tabq


def kernel(x, p, positions, g_mix, w_in, conv_w, conv_b, w_rg_a, b_rg_a, w_rg_x, b_rg_x, lru_L, g_q_lat, w_q_up, g_kv_lat, w_kv_up, g_out_rnn, g_out_att, w_out, g_ffn, w_ffn_gate, w_ffn_up, w_ffn_down, g_ple_in, w_ple_gate, w_ple_proj, g_ple_post, g_final):
    tabk, tabq = _rope_tables(positions)
    h = x
    depth = p.shape[0]
    assert depth == 1, "the mlp kernel applies the final norm, so only a single layer is supported"
    for l in range(depth):
        prm = _prep_layer(l, g_mix, w_in, conv_w, conv_b, w_rg_a, b_rg_a, w_rg_x, b_rg_x, lru_L,
                          g_q_lat, w_q_up, g_kv_lat, w_kv_up, g_out_rnn, g_out_att, w_out, g_ffn,
                          w_ffn_gate, w_ffn_up, w_ffn_down, g_ple_in, w_ple_gate, w_ple_proj,
                          g_ple_post, g_final)
        h = _layer(h, p[l], tabk, tabq, prm)
    return h
```

```python
import functools

import jax
import jax.numpy as jnp
from jax import lax
from jax.experimental import pallas as pl
from jax.experimental.pallas import tpu as pltpu

F32 = jnp.float32
BF16 = jnp.bfloat16

EPS = 1e-6
D_MODEL = 1024
D_RNN = 512
RNN_BLOCKS = 8
RNN_BW = 64
CONV_W = 4
LRU_C = 8.0
MLA_HEADS = 8
QK_NOPE = 64
QK_ROPE = 32
V_DIM = 64
Q_LORA = 256
KV_LORA = 128
D_ATT = MLA_HEADS * V_DIM
QK_DIM = QK_NOPE + QK_ROPE
ATT_SCALE = QK_DIM ** -0.5
ROPE_BASE = 10000.0
D_FF = 2816
PLE_DIM = 256

LANES = 128
SUBLANES = 8
HEAD_PAD = LANES
D_IN_EXT = 1536
FF_CHUNK = 256
VMEM_LIMIT = 56 * 1024 * 1024
ATT_HEADS_PER_STEP = 2
SUM_ROWS = 16
Q_SCALE = ATT_SCALE * 1.4426950408889634


def _rms(x, g):
    ms = jnp.mean(x * x, axis=-1, keepdims=True)
    return x * lax.rsqrt(ms + EPS) * g


def _dot(a, b):
    return jnp.dot(a, b, preferred_element_type=F32)


def _dot_nt(a, b):
    return lax.dot_general(a, b, (((1,), (1,)), ((), ())), preferred_element_type=F32)


def _proj_kernel(x_ref, tabk_ref, tabq_ref, gmix_ref, win_ref, gq_ref, wqt_ref, gkv_ref,
                 wk_ref, wvt_ref, zr_ref, qt_ref, k_ref, vt_ref, *, att_t):
    u = _rms(x_ref[...], gmix_ref[...]).astype(BF16)
    z = _dot(u, win_ref[...])
    zr_ref[...] = z[:, :2 * D_RNN].astype(BF16)
    c_q = z[:, 2 * D_RNN:2 * D_RNN + Q_LORA]
    c_kv = z[:, 2 * D_RNN + Q_LORA:2 * D_RNN + Q_LORA + KV_LORA]
    zk = z[:, 2 * D_RNN + Q_LORA + KV_LORA:]

    cqn = _rms(c_q, gq_ref[...]).astype(BF16)
    qt = _dot_nt(wqt_ref[...], cqn)
    half = QK_ROPE // 2
    cos_t = tabq_ref[0:half, :]
    sin_t = tabq_ref[half:QK_ROPE, :]
    for h in range(MLA_HEADS):
        base = h * HEAD_PAD
        r0 = base + QK_NOPE
        qt_ref[base:r0, :] = (qt[base:r0] * Q_SCALE).astype(BF16)
        t1 = qt[r0:r0 + half]
        t2 = qt[r0 + half:r0 + QK_ROPE]
        qt_ref[r0:r0 + half, :] = ((t1 * cos_t - t2 * sin_t) * Q_SCALE).astype(BF16)
        qt_ref[r0 + half:r0 + QK_ROPE, :] = ((t1 * sin_t + t2 * cos_t) * Q_SCALE).astype(BF16)
        qt_ref[r0 + QK_ROPE:base + HEAD_PAD, :] = jnp.zeros(
            (HEAD_PAD - QK_DIM, qt.shape[1]), BF16)

    ckvn = _rms(c_kv, gkv_ref[...]).astype(BF16)
    prod = zk * tabk_ref[...]
    kr = prod + pltpu.roll(prod, LANES - QK_ROPE, 1)
    kcat = jnp.concatenate([ckvn, kr.astype(BF16)], axis=1)
    k_ref[...] = _dot(kcat, wk_ref[...]).astype(BF16)
    vt = _dot_nt(wvt_ref[...], ckvn).astype(BF16)
    for c in range(vt.shape[1] // att_t):
        vt_ref[c] = vt[:, c * att_t:(c + 1) * att_t]


def _rglru_kernel(zr_ref, convw_ref, convb_ref, wg_ref, ba_ref, bx_ref, lru_ref, gout_ref,
                  y_ref, xpad_ref, a_ref, b_ref, carry_ref):
    t = zr_ref.shape[0]
    halo = SUBLANES

    @pl.when(pl.program_id(1) == 0)
    def _():
        xpad_ref[0:halo, :] = jnp.zeros((halo, D_RNN), F32)
        carry_ref[...] = jnp.zeros_like(carry_ref)

    xpad_ref[halo:, :] = zr_ref[:, :D_RNN].astype(F32)
    w = convw_ref[...]
    xc = convb_ref[...] + w[CONV_W - 1:CONV_W] * xpad_ref[halo:halo + t, :]
    for j in range(1, CONV_W):
        xc = xc + w[CONV_W - 1 - j:CONV_W - j] * xpad_ref[halo - j:halo - j + t, :]
    xpad_ref[0:halo, :] = xpad_ref[t:t + halo, :]

    neg_l = -lru_ref[...]
    sp = jnp.maximum(neg_l, 0.0) + jnp.log1p(jnp.exp(-jnp.abs(neg_l)))
    xcb = xc.astype(BF16)
    for g in range(D_RNN // LANES):
        sl = slice(g * LANES, (g + 1) * LANES)
        pre = _dot(xcb[:, sl], wg_ref[g])
        r = jax.nn.sigmoid(pre[:, :LANES] + ba_ref[:, sl])
        i = jax.nn.sigmoid(pre[:, LANES:] + bx_ref[:, sl])
        log_a = (-LRU_C * r) * sp[:, sl]
        a_ref[:, sl] = jnp.exp(log_a)
        th = jnp.tanh(log_a)
        b_ref[:, sl] = jnp.sqrt(-2.0 * th / (1.0 - th)) * (i * xc[:, sl])

    sub = lax.broadcasted_iota(jnp.int32, (SUBLANES, D_RNN), 0)

    def group(gi, carry):
        rows = pl.ds(pl.multiple_of(gi * SUBLANES, SUBLANES), SUBLANES)
        ag = a_ref[rows, :]
        bg = b_ref[rows, :]
        for d in (1, 2, 4):
            ash = jnp.where(sub >= d, pltpu.roll(ag, d, 0), 1.0)
            bsh = jnp.where(sub >= d, pltpu.roll(bg, d, 0), 0.0)
            bg = ag * bsh + bg
            ag = ag * ash
        hg = bg + ag * carry
        b_ref[rows, :] = hg
        return hg[SUBLANES - 1:SUBLANES, :]

    carry_ref[...] = lax.fori_loop(0, t // SUBLANES, group, carry_ref[...], unroll=4)

    y = b_ref[...] * jax.nn.gelu(zr_ref[:, D_RNN:].astype(F32))
    y_ref[...] = _rms(y, gout_ref[...]).astype(BF16)


def _attn_kernel(qt_ref, k_ref, vt_ref, o_ref, s_ref, *, att_t, heads):
    s_len = k_ref.shape[0]
    t = att_t
    kidx = lax.broadcasted_iota(jnp.int32, (t, t), 0)
    qidx = lax.broadcasted_iota(jnp.int32, (t, t), 1)
    causal = kidx <= qidx
    ones = jnp.ones((SUM_ROWS, t), BF16)

    def head_rows(hh, width):
        return slice(hh * width, (hh + 1) * width)

    def scores(slot, hh, kj, cols):
        qt = qt_ref[head_rows(hh, HEAD_PAD), cols]
        s_ref[slot, hh] = _dot(kj[:, head_rows(hh, HEAD_PAD)], qt)

    def consume(slot, hh, j, m, acc, masked):
        s = s_ref[slot, hh]
        if masked:
            s = jnp.where(causal, s, -jnp.inf)
        m_new = jnp.maximum(m, jnp.max(s, axis=0, keepdims=True))
        p = jnp.exp2(s - m_new).astype(BF16)
        alpha = jnp.exp2(m - m_new)
        vj = jnp.concatenate([vt_ref[j, head_rows(hh, V_DIM), :], ones], axis=0)
        return m_new, alpha * acc + _dot(vj, p)

    def finish(qi, slot, state):
        cols = slice(qi * t, (qi + 1) * t)
        for hh in range(heads):
            _, acc = consume(slot, hh, qi, state[2 * hh], state[2 * hh + 1], True)
            o_ref[head_rows(hh, V_DIM), cols] = acc[:V_DIM] / acc[V_DIM:V_DIM + 1]

    first = 0
    pending = None
    for qi in range(s_len // t):
        cols = slice(qi * t, (qi + 1) * t)
        k0 = k_ref[0:t, :]
        state = []
        for hh in range(heads):
            scores(first, hh, k0, cols)
            state += [jnp.full((1, t), -jnp.inf, F32), jnp.zeros((V_DIM + SUM_ROWS, t), F32)]
        if pending is not None:
            finish(*pending)

        def step(j, slot, carry, cols=cols):
            kn = k_ref[pl.ds(pl.multiple_of((j + 1) * t, t), t), :]
            out = []
            for hh in range(heads):
                scores(1 - slot, hh, kn, cols)
                out += consume(slot, hh, j, carry[2 * hh], carry[2 * hh + 1], False)
            return tuple(out)

        def pair(jj, carry, step=step, first=first):
            return step(2 * jj + 1, 1 - first, step(2 * jj, first, carry))

        state = lax.fori_loop(0, qi // 2, pair, tuple(state))
        if qi % 2:
            state = step(qi - 1, first, state)
        diag_slot = (first + qi) % 2
        pending = (qi, diag_slot, state)
        first = 1 - diag_slot
    finish(*pending)


def _mlp_kernel(x_ref, yr_ref, ot_ref, p_ref, gatt_ref, wo1_ref, wo2_ref, gffn_ref, wgu_ref,
                wd_ref, gpin_ref, wpg_ref, wpp_ref, gpost_ref, gfin_ref, out_ref,
                acc_ref, vff_ref):
    ya = _rms(ot_ref[...].T, gatt_ref[...]).astype(BF16)
    h = x_ref[...] + _dot(yr_ref[...], wo1_ref[...]) + _dot(ya, wo2_ref[...])
    acc_ref[...] = h
    vff_ref[...] = _rms(h, gffn_ref[...]).astype(BF16)

    def chunk(c, carry):
        gu = _dot(vff_ref[...], wgu_ref[c])
        act = (jax.nn.silu(gu[:, :FF_CHUNK]) * gu[:, FF_CHUNK:]).astype(BF16)
        acc_ref[...] += _dot(act, wd_ref[c])
        return carry

    lax.fori_loop(0, wgu_ref.shape[0], chunk, 0)
    h = acc_ref[...]
    e = _rms(_dot(p_ref[...].astype(BF16), wpp_ref[...]), gpost_ref[...])
    gate = jax.nn.sigmoid(_dot(_rms(h, gpin_ref[...]).astype(BF16), wpg_ref[...]))
    out_ref[...] = _rms(h + gate * e, gfin_ref[...])


def _const_spec(shape):
    zeros = (0,) * len(shape)
    return pl.BlockSpec(shape, lambda *_: zeros)


def _tiles(seq):
    proj_t = min(512, seq)
    att_t = min(512, seq)
    return proj_t, min(512, seq), att_t, min(512, seq)


def _layer(h, p_l, tabk, tabq, prm):
    b, s, _ = h.shape
    proj_t, rnn_t, att_t, mlp_t = _tiles(s)
    n = b * s
    cparams = functools.partial(pltpu.CompilerParams, vmem_limit_bytes=VMEM_LIMIT)

    zr, qt, k, vt = pl.pallas_call(
        functools.partial(_proj_kernel, att_t=att_t),
        grid=(b, s // proj_t),
        in_specs=[
            pl.BlockSpec((None, proj_t, D_MODEL), lambda i, j: (i, j, 0)),
            pl.BlockSpec((None, proj_t, LANES), lambda i, j: (i, j, 0)),
            pl.BlockSpec((None, QK_ROPE, proj_t), lambda i, j: (i, 0, j)),
            _const_spec((1, D_MODEL)),
            _const_spec((D_MODEL, D_IN_EXT)),
            _const_spec((1, Q_LORA)),
            _const_spec((MLA_HEADS * HEAD_PAD, Q_LORA)),
            _const_spec((1, KV_LORA)),
            _const_spec((2 * LANES, MLA_HEADS * HEAD_PAD)),
            _const_spec((D_ATT, KV_LORA)),
        ],
        out_specs=[
            pl.BlockSpec((None, proj_t, 2 * D_RNN), lambda i, j: (i, j, 0)),
            pl.BlockSpec((None, MLA_HEADS * HEAD_PAD, proj_t), lambda i, j: (i, 0, j)),
            pl.BlockSpec((None, proj_t, MLA_HEADS * HEAD_PAD), lambda i, j: (i, j, 0)),
            pl.BlockSpec((None, proj_t // att_t, D_ATT, att_t), lambda i, j: (i, j, 0, 0)),
        ],
        out_shape=[
            jax.ShapeDtypeStruct((b, s, 2 * D_RNN), BF16),
            jax.ShapeDtypeStruct((b, MLA_HEADS * HEAD_PAD, s), BF16),
            jax.ShapeDtypeStruct((b, s, MLA_HEADS * HEAD_PAD), BF16),
            jax.ShapeDtypeStruct((b, s // att_t, D_ATT, att_t), BF16),
        ],
        compiler_params=cparams(dimension_semantics=("arbitrary", "arbitrary")),
        name="proj",
    )(h, tabk, tabq, prm["g_mix"], prm["w_in"], prm["g_q"], prm["wq_t"], prm["g_kv"],
      prm["w_k"], prm["wv_t"])

    y_rnn = pl.pallas_call(
        _rglru_kernel,
        grid=(b, s // rnn_t),
        in_specs=[
            pl.BlockSpec((None, rnn_t, 2 * D_RNN), lambda i, j: (i, j, 0)),
            _const_spec((CONV_W, D_RNN)),
            _const_spec((1, D_RNN)),
            _const_spec((D_RNN // LANES, LANES, 2 * LANES)),
            _const_spec((1, D_RNN)),
            _const_spec((1, D_RNN)),
            _const_spec((1, D_RNN)),
            _const_spec((1, D_RNN)),
        ],
        out_specs=pl.BlockSpec((None, rnn_t, D_RNN), lambda i, j: (i, j, 0)),
        out_shape=jax.ShapeDtypeStruct((b, s, D_RNN), BF16),
        scratch_shapes=[
            pltpu.VMEM((rnn_t + SUBLANES, D_RNN), F32),
            pltpu.VMEM((rnn_t, D_RNN), F32),
            pltpu.VMEM((rnn_t, D_RNN), F32),
            pltpu.VMEM((1, D_RNN), F32),
        ],
        compiler_params=cparams(dimension_semantics=("arbitrary", "arbitrary")),
        name="rglru",
    )(zr, prm["conv_w"], prm["conv_b"], prm["w_gates"], prm["b_a"], prm["b_x"], prm["lru_l"],
      prm["g_out_rnn"])

    hg = ATT_HEADS_PER_STEP
    o_t = pl.pallas_call(
        functools.partial(_attn_kernel, att_t=att_t, heads=hg),
        grid=(b, MLA_HEADS // hg),
        in_specs=[
            pl.BlockSpec((None, hg * HEAD_PAD, s), lambda i, j: (i, j, 0)),
            pl.BlockSpec((None, s, hg * HEAD_PAD), lambda i, j: (i, 0, j)),
            pl.BlockSpec((None, s // att_t, hg * V_DIM, att_t), lambda i, j: (i, 0, j, 0)),
        ],
        out_specs=pl.BlockSpec((None, hg * V_DIM, s), lambda i, j: (i, j, 0)),
        out_shape=jax.ShapeDtypeStruct((b, D_ATT, s), F32),
        scratch_shapes=[pltpu.VMEM((2, hg, att_t, att_t), F32)],
        compiler_params=cparams(dimension_semantics=("arbitrary", "arbitrary")),
        name="attn",
    )(qt, k, vt)

    spt = s // mlp_t
    out = pl.pallas_call(
        _mlp_kernel,
        grid=(n // mlp_t,),
        in_specs=[
            pl.BlockSpec((mlp_t, D_MODEL), lambda i: (i, 0)),
            pl.BlockSpec((mlp_t, D_RNN), lambda i: (i, 0)),
            pl.BlockSpec((None, D_ATT, mlp_t), lambda i: (i // spt, 0, i % spt)),
            pl.BlockSpec((mlp_t, PLE_DIM), lambda i: (i, 0)),
            _const_spec((1, D_ATT)),
            _const_spec((D_RNN, D_MODEL)),
            _const_spec((D_ATT, D_MODEL)),
            _const_spec((1, D_MODEL)),
            _const_spec((D_FF // FF_CHUNK, D_MODEL, 2 * FF_CHUNK)),
            _const_spec((D_FF // FF_CHUNK, FF_CHUNK, D_MODEL)),
            _const_spec((1, D_MODEL)),
            _const_spec((D_MODEL, D_MODEL)),
            _const_spec((PLE_DIM, D_MODEL)),
            _const_spec((1, D_MODEL)),
            _const_spec((1, D_MODEL)),
        ],
        out_specs=pl.BlockSpec((mlp_t, D_MODEL), lambda i: (i, 0)),
        out_shape=jax.ShapeDtypeStruct((n, D_MODEL), F32),
        scratch_shapes=[
            pltpu.VMEM((mlp_t, D_MODEL), F32),
            pltpu.VMEM((mlp_t, D_MODEL), BF16),
        ],
        compiler_params=cparams(dimension_semantics=("arbitrary",)),
        name="mlp",
    )(h.reshape(n, D_MODEL), y_rnn.reshape(n, D_RNN), o_t, p_l.reshape(n, PLE_DIM),
      prm["g_out_att"], prm["w_out_rnn"], prm["w_out_att"], prm["g_ffn"], prm["w_gu"],
      prm["w_down"], prm["g_ple_in"], prm["w_ple_gate"], prm["w_ple_proj"], prm["g_ple_post"],
      prm["g_final"])
    return out.reshape(b, s, D_MODEL)


def _prep_layer(l, g_mix, w_in, conv_w, conv_b, w_rg_a, b_rg_a, w_rg_x, b_rg_x, lru_L, g_q_lat,
                w_q_up, g_kv_lat, w_kv_up, g_out_rnn, g_out_att, w_out, g_ffn, w_ffn_gate,
                w_ffn_up, w_ffn_down, g_ple_in, w_ple_gate, w_ple_proj, g_ple_post, g_final):
    row = lambda v: v.reshape(1, -1).astype(F32)
    half = QK_ROPE // 2
    w = w_in[l]
    w_kr = w[:, -QK_ROPE:]
    w_kr_rot = jnp.concatenate([-w_kr[:, half:], w_kr[:, :half]], axis=1)
    w_in_ext = jnp.concatenate(
        [w, w_kr_rot, jnp.zeros((D_MODEL, D_IN_EXT - w.shape[1] - QK_ROPE), F32)], axis=1)

    wq = w_q_up[l].reshape(Q_LORA, MLA_HEADS, QK_DIM)
    wq = jnp.pad(wq, ((0, 0), (0, 0), (0, HEAD_PAD - QK_DIM)))
    wq_t = wq.reshape(Q_LORA, MLA_HEADS * HEAD_PAD).T

    wkv = w_kv_up[l].reshape(KV_LORA, MLA_HEADS, QK_NOPE + V_DIM)
    wk_nope = jnp.pad(wkv[:, :, :QK_NOPE], ((0, 0), (0, 0), (0, HEAD_PAD - QK_NOPE)))
    wk_nope = wk_nope.reshape(KV_LORA, MLA_HEADS * HEAD_PAD)
    place = jnp.pad(jnp.eye(QK_ROPE, dtype=F32), ((0, LANES - QK_ROPE), (QK_NOPE, HEAD_PAD - QK_DIM)))
    w_k = jnp.concatenate([wk_nope, jnp.tile(place, (1, MLA_HEADS))], axis=0)
    wv_t = wkv[:, :, QK_NOPE:].reshape(KV_LORA, D_ATT).T

    def blockdiag(wb):
        wb = wb.reshape(D_RNN // LANES, 2, RNN_BW, RNN_BW)
        z = jnp.zeros_like(wb[:, 0])
        top = jnp.concatenate([wb[:, 0], z], axis=2)
        bot = jnp.concatenate([z, wb[:, 1]], axis=2)
        return jnp.concatenate([top, bot], axis=1)

    w_gates = jnp.concatenate([blockdiag(w_rg_a[l]), blockdiag(w_rg_x[l])], axis=2)

    nc = D_FF // FF_CHUNK
    wg = w_ffn_gate[l].reshape(D_MODEL, nc, FF_CHUNK)
    wu = w_ffn_up[l].reshape(D_MODEL, nc, FF_CHUNK)
    w_gu = jnp.concatenate([wg, wu], axis=2).transpose(1, 0, 2)

    return dict(
        g_mix=row(g_mix[l]), w_in=w_in_ext.astype(BF16), g_q=row(g_q_lat[l]),
        wq_t=wq_t.astype(BF16), g_kv=row(g_kv_lat[l]), w_k=w_k.astype(BF16),
        wv_t=wv_t.astype(BF16),
        conv_w=conv_w[l].astype(F32), conv_b=row(conv_b[l]), w_gates=w_gates.astype(BF16),
        b_a=row(b_rg_a[l]), b_x=row(b_rg_x[l]), lru_l=row(lru_L[l]), g_out_rnn=row(g_out_rnn[l]),
        g_out_att=row(g_out_att[l]), w_out_rnn=w_out[l][:D_RNN].astype(BF16),
        w_out_att=w_out[l][D_RNN:].astype(BF16), g_ffn=row(g_ffn[l]), w_gu=w_gu.astype(BF16),
        w_down=w_ffn_down[l].reshape(nc, FF_CHUNK, D_MODEL).astype(BF16),
        g_ple_in=row(g_ple_in[l]), w_ple_gate=w_ple_gate[l].astype(BF16),
        w_ple_proj=w_ple_proj[l].astype(BF16), g_ple_post=row(g_ple_post[l]),
        g_final=row(g_final),
    )


def _rope_tables(positions):
    half = QK_ROPE // 2
    inv = ROPE_BASE ** (-jnp.arange(0, QK_ROPE, 2, dtype=F32) / QK_ROPE)
    ang = positions.astype(F32)[..., None] * inv
    cos, sin = jnp.cos(ang), jnp.sin(ang)
    pad = jnp.zeros(ang.shape[:-1] + (LANES - 2 * QK_ROPE,), F32)
    tabk = jnp.concatenate([cos, cos, sin, sin, pad], axis=-1)
    tabq = jnp.concatenate([cos, sin], axis=-1).transpose(0, 2, 1)
    del half
    return tabk, tabq


def kernel(x, p, positions, g_mix, w_in, conv_w, conv_b, w_rg_a, b_rg_a, w_rg_x, b_rg_x, lru_L, g_q_lat, w_q_up, g_kv_lat, w_kv_up, g_out_rnn, g_out_att, w_out, g_ffn, w_ffn_gate, w_ffn_up, w_ffn_down, g_ple_in, w_ple_gate, w_ple_proj, g_ple_post, g_final):
    tabk, tabq = _rope_tables(positions)
    h = x
    depth = p.shape[0]
    assert depth == 1, "the mlp kernel applies the final norm, so only a single layer is supported"
    for l in range(depth):
        prm = _prep_layer(l, g_mix, w_in, conv_w, conv_b, w_rg_a, b_rg_a, w_rg_x, b_rg_x, lru_L,
                          g_q_lat, w_q_up, g_kv_lat, w_kv_up, g_out_rnn, g_out_att, w_out, g_ffn,
                          w_ffn_gate, w_ffn_up, w_ffn_down, g_ple_in, w_ple_gate, w_ple_proj,
                          g_ple_post, g_final)
        h = _layer(h, p[l], tabk, tabq, prm)
    return h
```

```python
import functools

import jax
import jax.numpy as jnp
from jax import lax
from jax.experimental import pallas as pl
from jax.experimental.pallas import tpu as pltpu

F32 = jnp.float32
BF16 = jnp.bfloat16

EPS = 1e-6
D_MODEL = 1024
D_RNN = 512
RNN_BLOCKS = 8
RNN_BW = 64
CONV_W = 4
LRU_C = 8.0
MLA_HEADS = 8
QK_NOPE = 64
QK_ROPE = 32
V_DIM = 64
Q_LORA = 256
KV_LORA = 128
D_ATT = MLA_HEADS * V_DIM
QK_DIM = QK_NOPE + QK_ROPE
ATT_SCALE = QK_DIM ** -0.5
ROPE_BASE = 10000.0
D_FF = 2816
PLE_DIM = 256

LANES = 128
SUBLANES = 8
HEAD_PAD = LANES
D_IN_EXT = 1536
FF_CHUNK = 256
FF_UNROLL = 4
VMEM_LIMIT = 56 * 1024 * 1024
ATT_HEADS_PER_STEP = 4
SUM_ROWS = 16
Q_SCALE = ATT_SCALE * 1.4426950408889634


def _rms(x, g):
    ms = jnp.mean(x * x, axis=-1, keepdims=True)
    return x * lax.rsqrt(ms + EPS) * g


def _dot(a, b):
    return jnp.dot(a, b, preferred_element_type=F32)


def _dot_nt(a, b):
    return lax.dot_general(a, b, (((1,), (1,)), ((), ())), preferred_element_type=F32)


def _proj_kernel(x_ref, pos_ref, inv_ref, gmix_ref, win_ref, gq_ref, wqt_ref, gkv_ref,
                 wk_ref, wvt_ref, zr_ref, qt_ref, k_ref, vt_ref, *, att_t):
    ang = inv_ref[...] * pos_ref[...].astype(F32)
    sin_t, cos_t = jnp.sin(ang), jnp.cos(ang)
    u = _rms(x_ref[...], gmix_ref[...]).astype(BF16)
    z = _dot(u, win_ref[...])
    zr_ref[...] = z[:, :2 * D_RNN].astype(BF16)
    c_q = z[:, 2 * D_RNN:2 * D_RNN + Q_LORA]
    c_kv = z[:, 2 * D_RNN + Q_LORA:2 * D_RNN + Q_LORA + KV_LORA]
    zk = z[:, 2 * D_RNN + Q_LORA + KV_LORA:]

    cqn = _rms(c_q, gq_ref[...]).astype(BF16)
    qt = _dot_nt(wqt_ref[...], cqn)
    half = QK_ROPE // 2
    for h in range(MLA_HEADS):
        base = h * HEAD_PAD
        r0 = base + QK_NOPE
        qt_ref[base:r0, :] = (qt[base:r0] * Q_SCALE).astype(BF16)
        t1 = qt[r0:r0 + half]
        t2 = qt[r0 + half:r0 + QK_ROPE]
        qt_ref[r0:r0 + half, :] = ((t1 * cos_t - t2 * sin_t) * Q_SCALE).astype(BF16)
        qt_ref[r0 + half:r0 + QK_ROPE, :] = ((t1 * sin_t + t2 * cos_t) * Q_SCALE).astype(BF16)
        qt_ref[r0 + QK_ROPE:base + HEAD_PAD, :] = jnp.zeros(
            (HEAD_PAD - QK_DIM, qt.shape[1]), BF16)

    ckvn = _rms(c_kv, gkv_ref[...]).astype(BF16)
    tabk = jnp.concatenate(
        [cos_t, cos_t, sin_t, sin_t, jnp.zeros((LANES - 2 * QK_ROPE, ang.shape[1]), F32)], axis=0).T
    prod = zk * tabk
    kr = prod + pltpu.roll(prod, LANES - QK_ROPE, 1)
    kcat = jnp.concatenate([ckvn, kr.astype(BF16)], axis=1)
    k_ref[...] = _dot(kcat, wk_ref[...]).astype(BF16)
    vt = _dot_nt(wvt_ref[...], ckvn).astype(BF16)
    for c in range(vt.shape[1] // att_t):
        vt_ref[c] = vt[:, c * att_t:(c + 1) * att_t]


def _sigmoid(x):
    return 0.5 * jnp.tanh(0.5 * x) + 0.5


def _gelu_tanh(x):
    c0 = (2.0 / jnp.pi) ** 0.5
    hx = 0.5 * x
    return hx + hx * jnp.tanh(x * (c0 + (c0 * 0.044715) * (x * x)))


def _rglru_kernel(zr_ref, convw_ref, convb_ref, wg_ref, ba_ref, bx_ref, lru_ref, gout_ref,
                  y_ref, xpad_ref, a_ref, b_ref, h_ref, *, nb):
    rows = zr_ref.shape[0]
    halo = (CONV_W - 1) * nb

    @pl.when(pl.program_id(0) == 0)
    def _():
        xpad_ref[0:halo, :] = jnp.zeros((halo, D_RNN), F32)
        h_ref[...] = jnp.zeros_like(h_ref)

    xpad_ref[halo:, :] = zr_ref[:, :D_RNN].astype(F32)
    w = convw_ref[...]
    xc = convb_ref[...] + w[CONV_W - 1:CONV_W] * xpad_ref[halo:, :]
    for j in range(1, CONV_W):
        xc = xc + w[CONV_W - 1 - j:CONV_W - j] * xpad_ref[halo - j * nb:halo - j * nb + rows, :]
    xpad_ref[0:halo, :] = xpad_ref[rows:, :]

    neg_l = -lru_ref[...]
    sp = jnp.maximum(neg_l, 0.0) + jnp.log1p(jnp.exp(-jnp.abs(neg_l)))
    xcb = xc.astype(BF16)
    for g in range(D_RNN // LANES):
        sl = slice(g * LANES, (g + 1) * LANES)
        pre = _dot(xcb[:, sl], wg_ref[g])
        r = _sigmoid(pre[:, :LANES] + ba_ref[:, sl])
        i = _sigmoid(pre[:, LANES:] + bx_ref[:, sl])
        neg_log_a = (LRU_C * r) * sp[:, sl]
        a = jnp.exp(-neg_log_a)
        a_ref[:, sl] = a
        q = jnp.tanh(neg_log_a) * (1.0 + a * a)
        mult = jnp.where(q > 0.0, q * lax.rsqrt(q), 0.0)
        b_ref[:, sl] = mult * (i * xc[:, sl])

    def step(ti, h):
        at = pl.ds(pl.multiple_of(ti * nb, nb), nb)
        h = a_ref[at, :] * h + b_ref[at, :]
        b_ref[at, :] = h
        return h

    h_ref[...] = lax.fori_loop(0, rows // nb, step, h_ref[...], unroll=8)

    y = b_ref[...] * _gelu_tanh(zr_ref[:, D_RNN:].astype(F32))
    y_ref[...] = _rms(y, gout_ref[...]).astype(BF16)


def _attn_kernel(qt_ref, k_ref, vt_ref, o_ref, s_ref, *, att_t, heads):
    s_len = k_ref.shape[0]
    t = att_t
    kidx = lax.broadcasted_iota(jnp.int32, (t, t), 0)
    qidx = lax.broadcasted_iota(jnp.int32, (t, t), 1)
    causal = kidx <= qidx
    ones = jnp.ones((SUM_ROWS, t), BF16)

    def head_rows(hh, width):
        return slice(hh * width, (hh + 1) * width)

    def scores(slot, hh, kj, cols):
        qt = qt_ref[head_rows(hh, HEAD_PAD), cols]
        s_ref[slot, hh] = _dot(kj[:, head_rows(hh, HEAD_PAD)], qt)

    def consume(slot, hh, j, m, acc, masked):
        s = s_ref[slot, hh]
        if masked:
            s = jnp.where(causal, s, -jnp.inf)
        m_new = jnp.maximum(m, jnp.max(s, axis=0, keepdims=True))
        p = jnp.exp2(s - m_new).astype(BF16)
        alpha = jnp.exp2(m - m_new)
        vj = jnp.concatenate([vt_ref[j, head_rows(hh, V_DIM), :], ones], axis=0)
        return m_new, alpha * acc + _dot(vj, p)

    def finish(qi, slot, state):
        cols = slice(qi * t, (qi + 1) * t)
        for hh in range(heads):
            _, acc = consume(slot, hh, qi, state[2 * hh], state[2 * hh + 1], True)
            o_ref[head_rows(hh, V_DIM), cols] = acc[:V_DIM] / acc[V_DIM:V_DIM + 1]

    first = 0
    pending = None
    for qi in range(s_len // t):
        cols = slice(qi * t, (qi + 1) * t)
        k0 = k_ref[0:t, :]
        state = []
        for hh in range(heads):
            scores(first, hh, k0, cols)
            state += [jnp.full((1, t), -jnp.inf, F32), jnp.zeros((V_DIM + SUM_ROWS, t), F32)]
        if pending is not None:
            finish(*pending)

        def step(j, slot, carry, cols=cols):
            kn = k_ref[pl.ds(pl.multiple_of((j + 1) * t, t), t), :]
            out = []
            for hh in range(heads):
                scores(1 - slot, hh, kn, cols)
                out += consume(slot, hh, j, carry[2 * hh], carry[2 * hh + 1], False)
            return tuple(out)

        def pair(jj, carry, step=step, first=first):
            return step(2 * jj + 1, 1 - first, step(2 * jj, first, carry))

        state = lax.fori_loop(0, qi // 2, pair, tuple(state))
        if qi % 2:
            state = step(qi - 1, first, state)
        diag_slot = (first + qi) % 2
        pending = (qi, diag_slot, state)
        first = 1 - diag_slot
    finish(*pending)


def _mlp_kernel(x_ref, yr_ref, ot_ref, p_ref, gatt_ref, wo1_ref, wo2_ref, gffn_ref, wg_ref,
                wu_ref, wd_ref, gpin_ref, wpg_ref, wpp_ref, gpost_ref, gfin_ref, out_ref,
                acc_ref, vff_ref):
    ya = _rms(ot_ref[...].T, gatt_ref[...]).astype(BF16)
    h = x_ref[...] + _dot(yr_ref[...], wo1_ref[...]) + _dot(ya, wo2_ref[...])
    acc_ref[...] = h
    vff_ref[...] = _rms(h, gffn_ref[...]).astype(BF16)

    def chunk(c, carry):
        cols = pl.ds(pl.multiple_of(c * FF_CHUNK, FF_CHUNK), FF_CHUNK)
        g = _dot(vff_ref[...], wg_ref[:, cols])
        u = _dot(vff_ref[...], wu_ref[:, cols])
        act = (jax.nn.silu(g) * u).astype(BF16)
        acc_ref[...] += _dot(act, wd_ref[cols, :])
        return carry

    lax.fori_loop(0, wg_ref.shape[1] // FF_CHUNK, chunk, 0, unroll=FF_UNROLL)
    h = acc_ref[...]
    e = _rms(_dot(p_ref[...].astype(BF16), wpp_ref[...]), gpost_ref[...])
    gate = jax.nn.sigmoid(_dot(_rms(h, gpin_ref[...]).astype(BF16), wpg_ref[...]))
    out_ref[...] = _rms(h + gate * e, gfin_ref[...])


def _const_spec(shape):
    zeros = (0,) * len(shape)
    return pl.BlockSpec(shape, lambda *_: zeros)


def _tiles(seq):
    proj_t = min(512, seq)
    att_t = min(512, seq)
    return proj_t, min(32, seq), att_t, min(512, seq)


def _layer(h, p_l, pos, inv, prm):
    b, s, _ = h.shape
    proj_t, rnn_t, att_t, mlp_t = _tiles(s)
    n = b * s
    cparams = functools.partial(pltpu.CompilerParams, vmem_limit_bytes=VMEM_LIMIT)

    zr, qt, k, vt = pl.pallas_call(
        functools.partial(_proj_kernel, att_t=att_t),
        grid=(b, s // proj_t),
        in_specs=[
            pl.BlockSpec((None, proj_t, D_MODEL), lambda i, j: (i, j, 0)),
            pl.BlockSpec((None, 1, proj_t), lambda i, j: (i, 0, j)),
            _const_spec((QK_ROPE // 2, 1)),
            _const_spec((1, D_MODEL)),
            _const_spec((D_MODEL, D_IN_EXT)),
            _const_spec((1, Q_LORA)),
            _const_spec((MLA_HEADS * HEAD_PAD, Q_LORA)),
            _const_spec((1, KV_LORA)),
            _const_spec((2 * LANES, MLA_HEADS * HEAD_PAD)),
            _const_spec((D_ATT, KV_LORA)),
        ],
        out_specs=[
            pl.BlockSpec((proj_t, 2 * D_RNN), lambda i, j: (j, i)),
            pl.BlockSpec((None, MLA_HEADS * HEAD_PAD, proj_t), lambda i, j: (i, 0, j)),
            pl.BlockSpec((None, proj_t, MLA_HEADS * HEAD_PAD), lambda i, j: (i, j, 0)),
            pl.BlockSpec((None, proj_t // att_t, D_ATT, att_t), lambda i, j: (i, j, 0, 0)),
        ],
        out_shape=[
            jax.ShapeDtypeStruct((s, b * 2 * D_RNN), BF16),
            jax.ShapeDtypeStruct((b, MLA_HEADS * HEAD_PAD, s), BF16),
            jax.ShapeDtypeStruct((b, s, MLA_HEADS * HEAD_PAD), BF16),
            jax.ShapeDtypeStruct((b, s // att_t, D_ATT, att_t), BF16),
        ],
        compiler_params=cparams(dimension_semantics=("arbitrary", "arbitrary")),
        name="proj",
    )(h, pos, inv, prm["g_mix"], prm["w_in"], prm["g_q"], prm["wq_t"], prm["g_kv"],
      prm["w_k"], prm["wv_t"])

    rnn_rows = rnn_t * b
    y_rnn = pl.pallas_call(
        functools.partial(_rglru_kernel, nb=b),
        grid=(s // rnn_t,),
        in_specs=[
            pl.BlockSpec((rnn_rows, 2 * D_RNN), lambda i: (i, 0)),
            _const_spec((CONV_W, D_RNN)),
            _const_spec((1, D_RNN)),
            _const_spec((D_RNN // LANES, LANES, 2 * LANES)),
            _const_spec((1, D_RNN)),
            _const_spec((1, D_RNN)),
            _const_spec((1, D_RNN)),
            _const_spec((1, D_RNN)),
        ],
        out_specs=pl.BlockSpec((rnn_rows, D_RNN), lambda i: (i, 0)),
        out_shape=jax.ShapeDtypeStruct((s * b, D_RNN), BF16),
        scratch_shapes=[
            pltpu.VMEM((rnn_rows + (CONV_W - 1) * b, D_RNN), F32),
            pltpu.VMEM((rnn_rows, D_RNN), F32),
            pltpu.VMEM((rnn_rows, D_RNN), F32),
            pltpu.VMEM((b, D_RNN), F32),
        ],
        compiler_params=cparams(dimension_semantics=("arbitrary",)),
        name="rglru",
    )(zr.reshape(s * b, 2 * D_RNN), prm["conv_w"], prm["conv_b"], prm["w_gates"], prm["b_a"], prm["b_x"], prm["lru_l"],
      prm["g_out_rnn"])

    hg = ATT_HEADS_PER_STEP
    o_t = pl.pallas_call(
        functools.partial(_attn_kernel, att_t=att_t, heads=hg),
        grid=(b, MLA_HEADS // hg),
        in_specs=[
            pl.BlockSpec((None, hg * HEAD_PAD, s), lambda i, j: (i, j, 0)),
            pl.BlockSpec((None, s, hg * HEAD_PAD), lambda i, j: (i, 0, j)),
            pl.BlockSpec((None, s // att_t, hg * V_DIM, att_t), lambda i, j: (i, 0, j, 0)),
        ],
        out_specs=pl.BlockSpec((None, hg * V_DIM, s), lambda i, j: (i, j, 0)),
        out_shape=jax.ShapeDtypeStruct((b, D_ATT, s), F32),
        scratch_shapes=[pltpu.VMEM((2, hg, att_t, att_t), F32)],
        compiler_params=cparams(dimension_semantics=("arbitrary", "arbitrary")),
        name="attn",
    )(qt, k, vt)

    spt = s // mlp_t
    out = pl.pallas_call(
        _mlp_kernel,
        grid=(n // mlp_t,),
        in_specs=[
            pl.BlockSpec((mlp_t, D_MODEL), lambda i: (i, 0)),
            pl.BlockSpec((mlp_t, D_RNN), lambda i: (i % spt, i // spt)),
            pl.BlockSpec((None, D_ATT, mlp_t), lambda i: (i // spt, 0, i % spt)),
            pl.BlockSpec((mlp_t, PLE_DIM), lambda i: (i, 0)),
            _const_spec((1, D_ATT)),
            _const_spec((D_RNN, D_MODEL)),
            _const_spec((D_ATT, D_MODEL)),
            _const_spec((1, D_MODEL)),
            _const_spec((D_MODEL, D_FF)),
            _const_spec((D_MODEL, D_FF)),
            _const_spec((D_FF, D_MODEL)),
            _const_spec((1, D_MODEL)),
            _const_spec((D_MODEL, D_MODEL)),
            _const_spec((PLE_DIM, D_MODEL)),
            _const_spec((1, D_MODEL)),
            _const_spec((1, D_MODEL)),
        ],
        out_specs=pl.BlockSpec((mlp_t, D_MODEL), lambda i: (i, 0)),
        out_shape=jax.ShapeDtypeStruct((n, D_MODEL), F32),
        scratch_shapes=[
            pltpu.VMEM((mlp_t, D_MODEL), F32),
            pltpu.VMEM((mlp_t, D_MODEL), BF16),
        ],
        compiler_params=cparams(dimension_semantics=("arbitrary",)),
        name="mlp",
    )(h.reshape(n, D_MODEL), y_rnn.reshape(s, b * D_RNN), o_t, p_l.reshape(n, PLE_DIM),
      prm["g_out_att"], prm["w_out_rnn"], prm["w_out_att"], prm["g_ffn"], prm["w_gate"],
      prm["w_up"], prm["w_down"], prm["g_ple_in"], prm["w_ple_gate"], prm["w_ple_proj"], prm["g_ple_post"],
      prm["g_final"])
    return out.reshape(b, s, D_MODEL)


def _prep_layer(l, g_mix, w_in, conv_w, conv_b, w_rg_a, b_rg_a, w_rg_x, b_rg_x, lru_L, g_q_lat,
                w_q_up, g_kv_lat, w_kv_up, g_out_rnn, g_out_att, w_out, g_ffn, w_ffn_gate,
                w_ffn_up, w_ffn_down, g_ple_in, w_ple_gate, w_ple_proj, g_ple_post, g_final):
    row = lambda v: v.reshape(1, -1).astype(F32)
    half = QK_ROPE // 2
    w = w_in[l]
    w_kr = w[:, -QK_ROPE:]
    w_kr_rot = jnp.concatenate([-w_kr[:, half:], w_kr[:, :half]], axis=1)
    w_in_ext = jnp.concatenate(
        [w, w_kr_rot, jnp.zeros((D_MODEL, D_IN_EXT - w.shape[1] - QK_ROPE), F32)], axis=1)

    wq = w_q_up[l].reshape(Q_LORA, MLA_HEADS, QK_DIM)
    wq = jnp.pad(wq, ((0, 0), (0, 0), (0, HEAD_PAD - QK_DIM)))
    wq_t = wq.reshape(Q_LORA, MLA_HEADS * HEAD_PAD).T

    wkv = w_kv_up[l].reshape(KV_LORA, MLA_HEADS, QK_NOPE + V_DIM)
    wk_nope = jnp.pad(wkv[:, :, :QK_NOPE], ((0, 0), (0, 0), (0, HEAD_PAD - QK_NOPE)))
    wk_nope = wk_nope.reshape(KV_LORA, MLA_HEADS * HEAD_PAD)
    place = jnp.pad(jnp.eye(QK_ROPE, dtype=F32), ((0, LANES - QK_ROPE), (QK_NOPE, HEAD_PAD - QK_DIM)))
    w_k = jnp.concatenate([wk_nope, jnp.tile(place, (1, MLA_HEADS))], axis=0)
    wv_t = wkv[:, :, QK_NOPE:].reshape(KV_LORA, D_ATT).T

    def blockdiag(wb):
        wb = wb.reshape(D_RNN // LANES, 2, RNN_BW, RNN_BW)
        z = jnp.zeros_like(wb[:, 0])
        top = jnp.concatenate([wb[:, 0], z], axis=2)
        bot = jnp.concatenate([z, wb[:, 1]], axis=2)
        return jnp.concatenate([top, bot], axis=1)

    w_gates = jnp.concatenate([blockdiag(w_rg_a[l]), blockdiag(w_rg_x[l])], axis=2)

    return dict(
        g_mix=row(g_mix[l]), w_in=w_in_ext.astype(BF16), g_q=row(g_q_lat[l]),
        wq_t=wq_t.astype(BF16), g_kv=row(g_kv_lat[l]), w_k=w_k.astype(BF16),
        wv_t=wv_t.astype(BF16),
        conv_w=conv_w[l].astype(F32), conv_b=row(conv_b[l]), w_gates=w_gates.astype(BF16),
        b_a=row(b_rg_a[l]), b_x=row(b_rg_x[l]), lru_l=row(lru_L[l]), g_out_rnn=row(g_out_rnn[l]),
        g_out_att=row(g_out_att[l]), w_out_rnn=w_out[l][:D_RNN].astype(BF16),
        w_out_att=w_out[l][D_RNN:].astype(BF16), g_ffn=row(g_ffn[l]),
        w_gate=w_ffn_gate[l].astype(BF16), w_up=w_ffn_up[l].astype(BF16),
        w_down=w_ffn_down[l].astype(BF16),
        g_ple_in=row(g_ple_in[l]), w_ple_gate=w_ple_gate[l].astype(BF16),
        w_ple_proj=w_ple_proj[l].astype(BF16), g_ple_post=row(g_ple_post[l]),
        g_final=row(g_final),
    )


def kernel(x, p, positions, g_mix, w_in, conv_w, conv_b, w_rg_a, b_rg_a, w_rg_x, b_rg_x, lru_L, g_q_lat, w_q_up, g_kv_lat, w_kv_up, g_out_rnn, g_out_att, w_out, g_ffn, w_ffn_gate, w_ffn_up, w_ffn_down, g_ple_in, w_ple_gate, w_ple_proj, g_ple_post, g_final):
    b, s = positions.shape
    pos = positions.reshape(b, 1, s)
    inv = (ROPE_BASE ** (-jnp.arange(0, QK_ROPE, 2, dtype=F32) / QK_ROPE)).reshape(-1, 1)
    h = x
    depth = p.shape[0]
    assert depth == 1, "the mlp kernel applies the final norm, so only a single layer is supported"
    for l in range(depth):
        prm = _prep_layer(l, g_mix, w_in, conv_w, conv_b, w_rg_a, b_rg_a, w_rg_x, b_rg_x, lru_L,
                          g_q_lat, w_q_up, g_kv_lat, w_kv_up, g_out_rnn, g_out_att, w_out, g_ffn,
                          w_ffn_gate, w_ffn_up, w_ffn_down, g_ple_in, w_ple_gate, w_ple_proj,
                          g_ple_post, g_final)
        h = _layer(h, p[l], pos, inv, prm)
    return h
```

```python
import functools

import jax
import jax.numpy as jnp
from jax import lax
from jax.experimental import pallas as pl
from jax.experimental.pallas import tpu as pltpu

F32 = jnp.float32
BF16 = jnp.bfloat16

EPS = 1e-6
D_MODEL = 1024
D_RNN = 512
RNN_BLOCKS = 8
RNN_BW = 64
CONV_W = 4
LRU_C = 8.0
MLA_HEADS = 8
QK_NOPE = 64
QK_ROPE = 32
V_DIM = 64
Q_LORA = 256
KV_LORA = 128
D_ATT = MLA_HEADS * V_DIM
QK_DIM = QK_NOPE + QK_ROPE
ATT_SCALE = QK_DIM ** -0.5
ROPE_BASE = 10000.0
D_FF = 2816
PLE_DIM = 256

LANES = 128
SUBLANES = 8
HEAD_PAD = LANES
D_IN_ATT = 512
RNN_TIME_TILE = 64
RNN_SUB_STEPS = 16
FF_CHUNK = 256
FF_UNROLL = 4
VMEM_LIMIT = 56 * 1024 * 1024
ATT_HEADS_PER_STEP = 4
SUM_ROWS = 16
Q_SCALE = ATT_SCALE * 1.4426950408889634


def _rms(x, g):
    ms = jnp.mean(x * x, axis=-1, keepdims=True)
    return x * lax.rsqrt(ms + EPS) * g


def _dot(a, b):
    return jnp.dot(a, b, preferred_element_type=F32)


def _dot_nt(a, b):
    return lax.dot_general(a, b, (((1,), (1,)), ((), ())), preferred_element_type=F32)


def _proj_kernel(x_ref, pos_ref, inv_ref, gmix_ref, win_ref, gq_ref, wqt_ref, gkv_ref,
                 wk_ref, wvt_ref, qt_ref, k_ref, vt_ref, *, att_t):
    ang = inv_ref[...] * pos_ref[...].astype(F32)
    sin_t, cos_t = jnp.sin(ang), jnp.cos(ang)
    u = _rms(x_ref[...], gmix_ref[...]).astype(BF16)
    z = _dot(u, win_ref[...])
    c_q = z[:, :Q_LORA]
    c_kv = z[:, Q_LORA:Q_LORA + KV_LORA]
    zk = z[:, Q_LORA + KV_LORA:]

    cqn = _rms(c_q, gq_ref[...]).astype(BF16)
    qt = _dot_nt(wqt_ref[...], cqn)
    half = QK_ROPE // 2
    for h in range(MLA_HEADS):
        base = h * HEAD_PAD
        r0 = base + QK_NOPE
        qt_ref[base:r0, :] = (qt[base:r0] * Q_SCALE).astype(BF16)
        t1 = qt[r0:r0 + half]
        t2 = qt[r0 + half:r0 + QK_ROPE]
        qt_ref[r0:r0 + half, :] = ((t1 * cos_t - t2 * sin_t) * Q_SCALE).astype(BF16)
        qt_ref[r0 + half:r0 + QK_ROPE, :] = ((t1 * sin_t + t2 * cos_t) * Q_SCALE).astype(BF16)
        qt_ref[r0 + QK_ROPE:base + HEAD_PAD, :] = jnp.zeros(
            (HEAD_PAD - QK_DIM, qt.shape[1]), BF16)

    ckvn = _rms(c_kv, gkv_ref[...]).astype(BF16)
    tabk = jnp.concatenate(
        [cos_t, cos_t, sin_t, sin_t, jnp.zeros((LANES - 2 * QK_ROPE, ang.shape[1]), F32)], axis=0).T
    prod = zk * tabk
    kr = prod + pltpu.roll(prod, LANES - QK_ROPE, 1)
    kcat = jnp.concatenate([ckvn, kr.astype(BF16)], axis=1)
    k_ref[...] = _dot(kcat, wk_ref[...]).astype(BF16)
    vt = _dot_nt(wvt_ref[...], ckvn).astype(BF16)
    for c in range(vt.shape[1] // att_t):
        vt_ref[c] = vt[:, c * att_t:(c + 1) * att_t]


def _sigmoid(x):
    return 0.5 * jnp.tanh(0.5 * x) + 0.5


def _gelu_tanh(x):
    c0 = (2.0 / jnp.pi) ** 0.5
    hx = 0.5 * x
    return hx + hx * jnp.tanh(x * (c0 + (c0 * 0.044715) * (x * x)))


def _rglru_kernel(x_ref, gmix_ref, win_ref, perm_ref, permt_ref, convw_ref, convb_ref, wg_ref,
                  ba_ref, bx_ref, lru_ref, gout_ref, y_ref, xpad_ref, gate_ref, a_ref, b_ref, h_ref):
    nb, tt, _ = x_ref.shape
    ts = min(RNN_SUB_STEPS, tt)
    sub = nb * ts
    halo = (CONV_W - 1) * nb

    @pl.when(pl.program_id(0) == 0)
    def _():
        xpad_ref[0:halo, :] = jnp.zeros((halo, D_RNN), F32)
        h_ref[...] = jnp.zeros_like(h_ref)

    neg_l = -lru_ref[...]
    sp = jnp.maximum(neg_l, 0.0) + jnp.log1p(jnp.exp(-jnp.abs(neg_l)))
    w = convw_ref[...]

    def project(q):
        xq = x_ref[:, q * ts:(q + 1) * ts, :].reshape(sub, D_MODEL)
        u = _rms(xq, gmix_ref[...]).astype(BF16)
        zr = _dot(perm_ref[...], _dot(u, win_ref[...]).astype(BF16))
        xpad_ref[halo + q * sub:halo + (q + 1) * sub, :] = zr[:, :D_RNN]
        gate_ref[q * sub:(q + 1) * sub, :] = zr[:, D_RNN:]

    def recur(q, h):
        lo = halo + q * sub
        xc = convb_ref[...] + w[CONV_W - 1:CONV_W] * xpad_ref[lo:lo + sub, :]
        for j in range(1, CONV_W):
            xc = xc + w[CONV_W - 1 - j:CONV_W - j] * xpad_ref[lo - j * nb:lo - j * nb + sub, :]
        xcb = xc.astype(BF16)
        for g in range(D_RNN // LANES):
            sl = slice(g * LANES, (g + 1) * LANES)
            pre = _dot(xcb[:, sl], wg_ref[g])
            r = _sigmoid(pre[:, :LANES] + ba_ref[:, sl])
            i = _sigmoid(pre[:, LANES:] + bx_ref[:, sl])
            neg_log_a = (LRU_C * r) * sp[:, sl]
            a = jnp.exp(-neg_log_a)
            a_ref[:, sl] = a
            qq = jnp.tanh(neg_log_a) * (1.0 + a * a)
            mult = jnp.where(qq > 0.0, qq * lax.rsqrt(qq), 0.0)
            b_ref[:, sl] = mult * (i * xc[:, sl])
        for ti in range(ts):
            at = slice(ti * nb, (ti + 1) * nb)
            h = a_ref[at, :] * h + b_ref[at, :]
            b_ref[at, :] = h
        y = _rms(b_ref[...] * _gelu_tanh(gate_ref[q * sub:(q + 1) * sub, :]), gout_ref[...])
        y = _dot(permt_ref[...], y.astype(BF16)).astype(BF16)
        y_ref[:, q * ts:(q + 1) * ts, :] = y.reshape(nb, ts, D_RNN)
        return h

    n_sub = tt // ts
    h = h_ref[...]
    project(0)
    for q in range(n_sub):
        if q + 1 < n_sub:
            project(q + 1)
        h = recur(q, h)
    h_ref[...] = h
    xpad_ref[0:halo, :] = xpad_ref[nb * tt:, :]


def _attn_kernel(qt_ref, k_ref, vt_ref, o_ref, s_ref, *, att_t, heads):
    s_len = k_ref.shape[0]
    t = att_t
    kidx = lax.broadcasted_iota(jnp.int32, (t, t), 0)
    qidx = lax.broadcasted_iota(jnp.int32, (t, t), 1)
    causal = kidx <= qidx
    ones = jnp.ones((SUM_ROWS, t), BF16)

    def head_rows(hh, width):
        return slice(hh * width, (hh + 1) * width)

    def scores(slot, hh, kj, cols):
        qt = qt_ref[head_rows(hh, HEAD_PAD), cols]
        s_ref[slot, hh] = _dot(kj[:, head_rows(hh, HEAD_PAD)], qt)

    def consume(slot, hh, j, m, acc, masked):
        s = s_ref[slot, hh]
        if masked:
            s = jnp.where(causal, s, -jnp.inf)
        m_new = jnp.maximum(m, jnp.max(s, axis=0, keepdims=True))
        p = jnp.exp2(s - m_new).astype(BF16)
        alpha = jnp.exp2(m - m_new)
        vj = jnp.concatenate([vt_ref[j, head_rows(hh, V_DIM), :], ones], axis=0)
        return m_new, alpha * acc + _dot(vj, p)

    def finish(qi, slot, state):
        cols = slice(qi * t, (qi + 1) * t)
        for hh in range(heads):
            _, acc = consume(slot, hh, qi, state[2 * hh], state[2 * hh + 1], True)
            o_ref[head_rows(hh, V_DIM), cols] = acc[:V_DIM] / acc[V_DIM:V_DIM + 1]

    first = 0
    pending = None
    for qi in range(s_len // t):
        cols = slice(qi * t, (qi + 1) * t)
        k0 = k_ref[0:t, :]
        state = []
        for hh in range(heads):
            scores(first, hh, k0, cols)
            state += [jnp.full((1, t), -jnp.inf, F32), jnp.zeros((V_DIM + SUM_ROWS, t), F32)]
        if pending is not None:
            finish(*pending)

        def step(j, slot, carry, cols=cols):
            kn = k_ref[pl.ds(pl.multiple_of((j + 1) * t, t), t), :]
            out = []
            for hh in range(heads):
                scores(1 - slot, hh, kn, cols)
                out += consume(slot, hh, j, carry[2 * hh], carry[2 * hh + 1], False)
            return tuple(out)

        def pair(jj, carry, step=step, first=first):
            return step(2 * jj + 1, 1 - first, step(2 * jj, first, carry))

        state = lax.fori_loop(0, qi // 2, pair, tuple(state))
        if qi % 2:
            state = step(qi - 1, first, state)
        diag_slot = (first + qi) % 2
        pending = (qi, diag_slot, state)
        first = 1 - diag_slot
    finish(*pending)


def _mlp_kernel(x_ref, yr_ref, ot_ref, p_ref, gatt_ref, wo1_ref, wo2_ref, gffn_ref, wg_ref,
                wu_ref, wd_ref, gpin_ref, wpg_ref, wpp_ref, gpost_ref, gfin_ref, out_ref,
                acc_ref, vff_ref):
    ya = _rms(ot_ref[...].T, gatt_ref[...]).astype(BF16)
    h = x_ref[...] + _dot(yr_ref[...], wo1_ref[...]) + _dot(ya, wo2_ref[...])
    acc_ref[...] = h
    vff_ref[...] = _rms(h, gffn_ref[...]).astype(BF16)

    def chunk(c, carry):
        cols = pl.ds(pl.multiple_of(c * FF_CHUNK, FF_CHUNK), FF_CHUNK)
        g = _dot(vff_ref[...], wg_ref[:, cols])
        u = _dot(vff_ref[...], wu_ref[:, cols])
        act = (jax.nn.silu(g) * u).astype(BF16)
        acc_ref[...] += _dot(act, wd_ref[cols, :])
        return carry

    lax.fori_loop(0, wg_ref.shape[1] // FF_CHUNK, chunk, 0, unroll=FF_UNROLL)
    h = acc_ref[...]
    e = _rms(_dot(p_ref[...].astype(BF16), wpp_ref[...]), gpost_ref[...])
    gate = jax.nn.sigmoid(_dot(_rms(h, gpin_ref[...]).astype(BF16), wpg_ref[...]))
    out_ref[...] = _rms(h + gate * e, gfin_ref[...])


def _const_spec(shape):
    zeros = (0,) * len(shape)
    return pl.BlockSpec(shape, lambda *_: zeros)


def _tiles(seq):
    proj_t = min(512, seq)
    att_t = min(512, seq)
    return proj_t, min(RNN_TIME_TILE, seq), att_t, min(512, seq)


def _layer(h, p_l, pos, inv, prm):
    b, s, _ = h.shape
    proj_t, rnn_t, att_t, mlp_t = _tiles(s)
    n = b * s
    cparams = functools.partial(pltpu.CompilerParams, vmem_limit_bytes=VMEM_LIMIT)

    qt, k, vt = pl.pallas_call(
        functools.partial(_proj_kernel, att_t=att_t),
        grid=(b, s // proj_t),
        in_specs=[
            pl.BlockSpec((None, proj_t, D_MODEL), lambda i, j: (i, j, 0)),
            pl.BlockSpec((None, 1, proj_t), lambda i, j: (i, 0, j)),
            _const_spec((QK_ROPE // 2, 1)),
            _const_spec((1, D_MODEL)),
            _const_spec((D_MODEL, D_IN_ATT)),
            _const_spec((1, Q_LORA)),
            _const_spec((MLA_HEADS * HEAD_PAD, Q_LORA)),
            _const_spec((1, KV_LORA)),
            _const_spec((2 * LANES, MLA_HEADS * HEAD_PAD)),
            _const_spec((D_ATT, KV_LORA)),
        ],
        out_specs=[
            pl.BlockSpec((None, MLA_HEADS * HEAD_PAD, proj_t), lambda i, j: (i, 0, j)),
            pl.BlockSpec((None, proj_t, MLA_HEADS * HEAD_PAD), lambda i, j: (i, j, 0)),
            pl.BlockSpec((None, proj_t // att_t, D_ATT, att_t), lambda i, j: (i, j, 0, 0)),
        ],
        out_shape=[
            jax.ShapeDtypeStruct((b, MLA_HEADS * HEAD_PAD, s), BF16),
            jax.ShapeDtypeStruct((b, s, MLA_HEADS * HEAD_PAD), BF16),
            jax.ShapeDtypeStruct((b, s // att_t, D_ATT, att_t), BF16),
        ],
        compiler_params=cparams(dimension_semantics=("arbitrary", "arbitrary")),
        name="proj",
    )(h, pos, inv, prm["g_mix"], prm["w_in_att"], prm["g_q"], prm["wq_t"], prm["g_kv"],
      prm["w_k"], prm["wv_t"])

    rnn_rows = rnn_t * b
    sub_t = min(RNN_SUB_STEPS, rnn_t)
    sub_rows = sub_t * b
    tb = jnp.arange(sub_rows)
    perm = ((tb % b) * sub_t + tb // b)[:, None] == tb[None, :]
    perm = perm.astype(BF16)
    y_rnn = pl.pallas_call(
        _rglru_kernel,
        grid=(s // rnn_t,),
        in_specs=[
            pl.BlockSpec((b, rnn_t, D_MODEL), lambda i: (0, i, 0)),
            _const_spec((1, D_MODEL)),
            _const_spec((D_MODEL, 2 * D_RNN)),
            _const_spec((sub_rows, sub_rows)),
            _const_spec((sub_rows, sub_rows)),
            _const_spec((CONV_W, D_RNN)),
            _const_spec((1, D_RNN)),
            _const_spec((D_RNN // LANES, LANES, 2 * LANES)),
            _const_spec((1, D_RNN)),
            _const_spec((1, D_RNN)),
            _const_spec((1, D_RNN)),
            _const_spec((1, D_RNN)),
        ],
        out_specs=pl.BlockSpec((b, rnn_t, D_RNN), lambda i: (0, i, 0)),
        out_shape=jax.ShapeDtypeStruct((b, s, D_RNN), BF16),
        scratch_shapes=[
            pltpu.VMEM((rnn_rows + (CONV_W - 1) * b, D_RNN), F32),
            pltpu.VMEM((rnn_rows, D_RNN), F32),
            pltpu.VMEM((sub_rows, D_RNN), F32),
            pltpu.VMEM((sub_rows, D_RNN), F32),
            pltpu.VMEM((b, D_RNN), F32),
        ],
        compiler_params=cparams(dimension_semantics=("arbitrary",)),
        name="rglru",
    )(h, prm["g_mix"], prm["w_in_rnn"], perm, perm.T, prm["conv_w"], prm["conv_b"],
      prm["w_gates"], prm["b_a"], prm["b_x"], prm["lru_l"], prm["g_out_rnn"])

    hg = ATT_HEADS_PER_STEP
    o_t = pl.pallas_call(
        functools.partial(_attn_kernel, att_t=att_t, heads=hg),
        grid=(b, MLA_HEADS // hg),
        in_specs=[
            pl.BlockSpec((None, hg * HEAD_PAD, s), lambda i, j: (i, j, 0)),
            pl.BlockSpec((None, s, hg * HEAD_PAD), lambda i, j: (i, 0, j)),
            pl.BlockSpec((None, s // att_t, hg * V_DIM, att_t), lambda i, j: (i, 0, j, 0)),
        ],
        out_specs=pl.BlockSpec((None, hg * V_DIM, s), lambda i, j: (i, j, 0)),
        out_shape=jax.ShapeDtypeStruct((b, D_ATT, s), F32),
        scratch_shapes=[pltpu.VMEM((2, hg, att_t, att_t), F32)],
        compiler_params=cparams(dimension_semantics=("arbitrary", "arbitrary")),
        name="attn",
    )(qt, k, vt)

    spt = s // mlp_t
    out = pl.pallas_call(
        _mlp_kernel,
        grid=(n // mlp_t,),
        in_specs=[
            pl.BlockSpec((mlp_t, D_MODEL), lambda i: (i, 0)),
            pl.BlockSpec((mlp_t, D_RNN), lambda i: (i, 0)),
            pl.BlockSpec((None, D_ATT, mlp_t), lambda i: (i // spt, 0, i % spt)),
            pl.BlockSpec((mlp_t, PLE_DIM), lambda i: (i, 0)),
            _const_spec((1, D_ATT)),
            _const_spec((D_RNN, D_MODEL)),
            _const_spec((D_ATT, D_MODEL)),
            _const_spec((1, D_MODEL)),
            _const_spec((D_MODEL, D_FF)),
            _const_spec((D_MODEL, D_FF)),
            _const_spec((D_FF, D_MODEL)),
            _const_spec((1, D_MODEL)),
            _const_spec((D_MODEL, D_MODEL)),
            _const_spec((PLE_DIM, D_MODEL)),
            _const_spec((1, D_MODEL)),
            _const_spec((1, D_MODEL)),
        ],
        out_specs=pl.BlockSpec((mlp_t, D_MODEL), lambda i: (i, 0)),
        out_shape=jax.ShapeDtypeStruct((n, D_MODEL), F32),
        scratch_shapes=[
            pltpu.VMEM((mlp_t, D_MODEL), F32),
            pltpu.VMEM((mlp_t, D_MODEL), BF16),
        ],
        compiler_params=cparams(dimension_semantics=("arbitrary",)),
        name="mlp",
    )(h.reshape(n, D_MODEL), y_rnn.reshape(n, D_RNN), o_t, p_l.reshape(n, PLE_DIM),
      prm["g_out_att"], prm["w_out_rnn"], prm["w_out_att"], prm["g_ffn"], prm["w_gate"],
      prm["w_up"], prm["w_down"], prm["g_ple_in"], prm["w_ple_gate"], prm["w_ple_proj"], prm["g_ple_post"],
      prm["g_final"])
    return out.reshape(b, s, D_MODEL)


def _prep_layer(l, g_mix, w_in, conv_w, conv_b, w_rg_a, b_rg_a, w_rg_x, b_rg_x, lru_L, g_q_lat,
                w_q_up, g_kv_lat, w_kv_up, g_out_rnn, g_out_att, w_out, g_ffn, w_ffn_gate,
                w_ffn_up, w_ffn_down, g_ple_in, w_ple_gate, w_ple_proj, g_ple_post, g_final):
    row = lambda v: v.reshape(1, -1).astype(F32)
    half = QK_ROPE // 2
    w = w_in[l]
    w_kr = w[:, -QK_ROPE:]
    w_kr_rot = jnp.concatenate([-w_kr[:, half:], w_kr[:, :half]], axis=1)
    w_att = w[:, 2 * D_RNN:]
    w_in_att = jnp.concatenate(
        [w_att, w_kr_rot, jnp.zeros((D_MODEL, D_IN_ATT - w_att.shape[1] - QK_ROPE), F32)], axis=1)

    wq = w_q_up[l].reshape(Q_LORA, MLA_HEADS, QK_DIM)
    wq = jnp.pad(wq, ((0, 0), (0, 0), (0, HEAD_PAD - QK_DIM)))
    wq_t = wq.reshape(Q_LORA, MLA_HEADS * HEAD_PAD).T

    wkv = w_kv_up[l].reshape(KV_LORA, MLA_HEADS, QK_NOPE + V_DIM)
    wk_nope = jnp.pad(wkv[:, :, :QK_NOPE], ((0, 0), (0, 0), (0, HEAD_PAD - QK_NOPE)))
    wk_nope = wk_nope.reshape(KV_LORA, MLA_HEADS * HEAD_PAD)
    place = jnp.pad(jnp.eye(QK_ROPE, dtype=F32), ((0, LANES - QK_ROPE), (QK_NOPE, HEAD_PAD - QK_DIM)))
    w_k = jnp.concatenate([wk_nope, jnp.tile(place, (1, MLA_HEADS))], axis=0)
    wv_t = wkv[:, :, QK_NOPE:].reshape(KV_LORA, D_ATT).T

    def blockdiag(wb):
        wb = wb.reshape(D_RNN // LANES, 2, RNN_BW, RNN_BW)
        z = jnp.zeros_like(wb[:, 0])
        top = jnp.concatenate([wb[:, 0], z], axis=2)
        bot = jnp.concatenate([z, wb[:, 1]], axis=2)
        return jnp.concatenate([top, bot], axis=1)

    w_gates = jnp.concatenate([blockdiag(w_rg_a[l]), blockdiag(w_rg_x[l])], axis=2)

    return dict(
        g_mix=row(g_mix[l]), w_in_att=w_in_att.astype(BF16),
        w_in_rnn=w[:, :2 * D_RNN].astype(BF16), g_q=row(g_q_lat[l]),
        wq_t=wq_t.astype(BF16), g_kv=row(g_kv_lat[l]), w_k=w_k.astype(BF16),
        wv_t=wv_t.astype(BF16),
        conv_w=conv_w[l].astype(F32), conv_b=row(conv_b[l]), w_gates=w_gates.astype(BF16),
        b_a=row(b_rg_a[l]), b_x=row(b_rg_x[l]), lru_l=row(lru_L[l]), g_out_rnn=row(g_out_rnn[l]),
        g_out_att=row(g_out_att[l]), w_out_rnn=w_out[l][:D_RNN].astype(BF16),
        w_out_att=w_out[l][D_RNN:].astype(BF16), g_ffn=row(g_ffn[l]),
        w_gate=w_ffn_gate[l].astype(BF16), w_up=w_ffn_up[l].astype(BF16),
        w_down=w_ffn_down[l].astype(BF16),
        g_ple_in=row(g_ple_in[l]), w_ple_gate=w_ple_gate[l].astype(BF16),
        w_ple_proj=w_ple_proj[l].astype(BF16), g_ple_post=row(g_ple_post[l]),
        g_final=row(g_final),
    )


def kernel(x, p, positions, g_mix, w_in, conv_w, conv_b, w_rg_a, b_rg_a, w_rg_x, b_rg_x, lru_L, g_q_lat, w_q_up, g_kv_lat, w_kv_up, g_out_rnn, g_out_att, w_out, g_ffn, w_ffn_gate, w_ffn_up, w_ffn_down, g_ple_in, w_ple_gate, w_ple_proj, g_ple_post, g_final):
    b, s = positions.shape
    pos = positions.reshape(b, 1, s)
    inv = (ROPE_BASE ** (-jnp.arange(0, QK_ROPE, 2, dtype=F32) / QK_ROPE)).reshape(-1, 1)
    h = x
    depth = p.shape[0]
    assert depth == 1, "the mlp kernel applies the final norm, so only a single layer is supported"
    for l in range(depth):
        prm = _prep_layer(l, g_mix, w_in, conv_w, conv_b, w_rg_a, b_rg_a, w_rg_x, b_rg_x, lru_L,
                          g_q_lat, w_q_up, g_kv_lat, w_kv_up, g_out_rnn, g_out_att, w_out, g_ffn,
                          w_ffn_gate, w_ffn_up, w_ffn_down, g_ple_in, w_ple_gate, w_ple_proj,
                          g_ple_post, g_final)
        h = _layer(h, p[l], pos, inv, prm)
    return h
```

```python
import functools

import jax
import jax.numpy as jnp
from jax import lax
from jax.experimental import pallas as pl
from jax.experimental.pallas import tpu as pltpu

F32 = jnp.float32
BF16 = jnp.bfloat16

EPS = 1e-6
D_MODEL = 1024
D_RNN = 512
RNN_BLOCKS = 8
RNN_BW = 64
CONV_W = 4
LRU_C = 8.0
MLA_HEADS = 8
QK_NOPE = 64
QK_ROPE = 32
V_DIM = 64
Q_LORA = 256
KV_LORA = 128
D_ATT = MLA_HEADS * V_DIM
QK_DIM = QK_NOPE + QK_ROPE
ATT_SCALE = QK_DIM ** -0.5
ROPE_BASE = 10000.0
D_FF = 2816
PLE_DIM = 256

LANES = 128
SUBLANES = 8
HEAD_PAD = LANES
D_IN_ATT = 512
PROJ_SUB_TILE = 512
RNN_TIME_TILE = 64
RNN_SUB_STEPS = 16
FF_CHUNK = 256
FF_UNROLL = 4
VMEM_LIMIT = 56 * 1024 * 1024
ATT_HEADS_PER_STEP = 4
SUM_ROWS = 16
Q_SCALE = ATT_SCALE * 1.4426950408889634


def _rms(x, g):
    ms = jnp.mean(x * x, axis=-1, keepdims=True)
    return x * lax.rsqrt(ms + EPS) * g


def _dot(a, b):
    return jnp.dot(a, b, preferred_element_type=F32)


def _dot_nt(a, b):
    return lax.dot_general(a, b, (((1,), (1,)), ((), ())), preferred_element_type=F32)


def _proj_kernel(x_ref, pos_ref, inv_ref, gmix_ref, win_ref, gq_ref, wqt_ref, gkv_ref,
                 wk_ref, wvt_ref, qt_ref, k_ref, vt_ref, *, att_t):
    half = QK_ROPE // 2
    sub = min(PROJ_SUB_TILE, att_t)
    for i in range(x_ref.shape[0] // sub):
        rows = slice(i * sub, (i + 1) * sub)
        ang = inv_ref[...] * pos_ref[:, rows].astype(F32)
        sin_t, cos_t = jnp.sin(ang), jnp.cos(ang)
        u = _rms(x_ref[rows, :], gmix_ref[...]).astype(BF16)
        z = _dot(u, win_ref[...])
        c_q = z[:, :Q_LORA]
        c_kv = z[:, Q_LORA:Q_LORA + KV_LORA]
        zk = z[:, Q_LORA + KV_LORA:]

        cqn = _rms(c_q, gq_ref[...]).astype(BF16)
        qt = _dot_nt(wqt_ref[...], cqn)
        for h in range(MLA_HEADS):
            base = h * HEAD_PAD
            r0 = base + QK_NOPE
            qt_ref[base:r0, rows] = (qt[base:r0] * Q_SCALE).astype(BF16)
            t1 = qt[r0:r0 + half]
            t2 = qt[r0 + half:r0 + QK_ROPE]
            qt_ref[r0:r0 + half, rows] = ((t1 * cos_t - t2 * sin_t) * Q_SCALE).astype(BF16)
            qt_ref[r0 + half:r0 + QK_ROPE, rows] = ((t1 * sin_t + t2 * cos_t) * Q_SCALE).astype(BF16)
            qt_ref[r0 + QK_ROPE:base + HEAD_PAD, rows] = jnp.zeros((HEAD_PAD - QK_DIM, sub), BF16)

        ckvn = _rms(c_kv, gkv_ref[...]).astype(BF16)
        tabk = jnp.concatenate(
            [cos_t, cos_t, sin_t, sin_t, jnp.zeros((LANES - 2 * QK_ROPE, sub), F32)], axis=0).T
        prod = zk * tabk
        kr = prod + pltpu.roll(prod, LANES - QK_ROPE, 1)
        kcat = jnp.concatenate([ckvn, kr.astype(BF16)], axis=1)
        k_ref[rows, :] = _dot(kcat, wk_ref[...]).astype(BF16)
        vt = _dot_nt(wvt_ref[...], ckvn).astype(BF16)
        start = i * sub
        vt_ref[start // att_t, :, start % att_t:start % att_t + sub] = vt


def _sigmoid(x):
    return 0.5 * jnp.tanh(0.5 * x) + 0.5


def _gelu_tanh(x):
    c0 = (2.0 / jnp.pi) ** 0.5
    hx = 0.5 * x
    return hx + hx * jnp.tanh(x * (c0 + (c0 * 0.044715) * (x * x)))


def _rglru_kernel(x_ref, gmix_ref, win_ref, perm_ref, permt_ref, convw_ref, convb_ref, wg_ref,
                  ba_ref, bx_ref, lru_ref, gout_ref, y_ref, xpad_ref, gate_ref, a_ref, b_ref, h_ref):
    nb, tt, _ = x_ref.shape
    ts = min(RNN_SUB_STEPS, tt)
    sub = nb * ts
    halo = (CONV_W - 1) * nb

    @pl.when(pl.program_id(0) == 0)
    def _():
        xpad_ref[0:halo, :] = jnp.zeros((halo, D_RNN), F32)
        h_ref[...] = jnp.zeros_like(h_ref)

    neg_l = -lru_ref[...]
    sp = jnp.maximum(neg_l, 0.0) + jnp.log1p(jnp.exp(-jnp.abs(neg_l)))
    w = convw_ref[...]

    def project(q):
        xq = x_ref[:, q * ts:(q + 1) * ts, :].reshape(sub, D_MODEL)
        u = _rms(xq, gmix_ref[...]).astype(BF16)
        zr = _dot(perm_ref[...], _dot(u, win_ref[...]).astype(BF16))
        xpad_ref[halo + q * sub:halo + (q + 1) * sub, :] = zr[:, :D_RNN]
        gate_ref[q * sub:(q + 1) * sub, :] = zr[:, D_RNN:]

    def recur(q, h):
        lo = halo + q * sub
        xc = convb_ref[...] + w[CONV_W - 1:CONV_W] * xpad_ref[lo:lo + sub, :]
        for j in range(1, CONV_W):
            xc = xc + w[CONV_W - 1 - j:CONV_W - j] * xpad_ref[lo - j * nb:lo - j * nb + sub, :]
        xcb = xc.astype(BF16)
        for g in range(D_RNN // LANES):
            sl = slice(g * LANES, (g + 1) * LANES)
            pre = _dot(xcb[:, sl], wg_ref[g])
            r = _sigmoid(pre[:, :LANES] + ba_ref[:, sl])
            i = _sigmoid(pre[:, LANES:] + bx_ref[:, sl])
            neg_log_a = (LRU_C * r) * sp[:, sl]
            a = jnp.exp(-neg_log_a)
            a_ref[:, sl] = a
            qq = jnp.tanh(neg_log_a) * (1.0 + a * a)
            mult = jnp.where(qq > 0.0, qq * lax.rsqrt(qq), 0.0)
            b_ref[:, sl] = mult * (i * xc[:, sl])
        for ti in range(ts):
            at = slice(ti * nb, (ti + 1) * nb)
            h = a_ref[at, :] * h + b_ref[at, :]
            b_ref[at, :] = h
        y = _rms(b_ref[...] * _gelu_tanh(gate_ref[q * sub:(q + 1) * sub, :]), gout_ref[...])
        y = _dot(permt_ref[...], y.astype(BF16)).astype(BF16)
        y_ref[:, q * ts:(q + 1) * ts, :] = y.reshape(nb, ts, D_RNN)
        return h

    n_sub = tt // ts
    h = h_ref[...]
    project(0)
    for q in range(n_sub):
        if q + 1 < n_sub:
            project(q + 1)
        h = recur(q, h)
    h_ref[...] = h
    xpad_ref[0:halo, :] = xpad_ref[nb * tt:, :]


def _attn_kernel(qt_ref, k_ref, vt_ref, o_ref, s_ref, *, att_t, heads):
    s_len = k_ref.shape[0]
    t = att_t
    h2 = t // 2
    kidx = lax.broadcasted_iota(jnp.int32, (t, t), 0)
    qidx = lax.broadcasted_iota(jnp.int32, (t, t), 1)
    causal = kidx <= qidx
    ones = jnp.ones((SUM_ROWS, t), BF16)

    def head_rows(hh, width):
        return slice(hh * width, (hh + 1) * width)

    def scores(slot, hh, kj, cols):
        qt = qt_ref[head_rows(hh, HEAD_PAD), cols]
        s_ref[slot, hh] = _dot(kj[:, head_rows(hh, HEAD_PAD)], qt)

    def diag_scores(slot, hh, qi):
        lo = qi * t
        qt = qt_ref[head_rows(hh, HEAD_PAD), lo:lo + t]
        s_ref[slot, hh, 0:h2, :] = _dot(k_ref[lo:lo + h2, head_rows(hh, HEAD_PAD)], qt)
        s_ref[slot, hh, h2:t, h2:t] = _dot(k_ref[lo + h2:lo + t, head_rows(hh, HEAD_PAD)], qt[:, h2:])

    def softmax_step(s, m, acc, vj):
        m_new = jnp.maximum(m, jnp.max(s, axis=0, keepdims=True))
        p = jnp.exp2(s - m_new).astype(BF16)
        return m_new, jnp.exp2(m - m_new) * acc + _dot(vj, p)

    def values(hh, j):
        return jnp.concatenate([vt_ref[j, head_rows(hh, V_DIM), :], ones], axis=0)

    def consume(slot, hh, j, m, acc):
        return softmax_step(s_ref[slot, hh], m, acc, values(hh, j))

    def finish(qi, slot, state):
        for hh in range(heads):
            m, acc = state[2 * hh], state[2 * hh + 1]
            vj = values(hh, qi)
            halves = (
                (slice(0, h2), slice(0, h2)),
                (slice(0, t), slice(h2, t)),
            )
            for keys, qcols in halves:
                s = jnp.where(causal[keys, qcols], s_ref[slot, hh, keys, qcols], -jnp.inf)
                _, a = softmax_step(s, m[:, qcols], acc[:, qcols], vj[:, keys])
                o_ref[head_rows(hh, V_DIM), qi * t + qcols.start:qi * t + qcols.stop] = (
                    a[:V_DIM] / a[V_DIM:V_DIM + 1])

    first = 0
    pending = None
    for qi in range(s_len // t):
        cols = slice(qi * t, (qi + 1) * t)
        k0 = k_ref[0:t, :]
        state = []
        for hh in range(heads):
            if qi == 0:
                diag_scores(first, hh, qi)
            else:
                scores(first, hh, k0, cols)
            state += [jnp.full((1, t), -jnp.inf, F32), jnp.zeros((V_DIM + SUM_ROWS, t), F32)]
        if pending is not None:
            finish(*pending)

        def step(j, slot, carry, last=False, cols=cols, qi=qi):
            if not last:
                kn = k_ref[pl.ds(pl.multiple_of((j + 1) * t, t), t), :]
            out = []
            for hh in range(heads):
                if last:
                    diag_scores(1 - slot, hh, qi)
                else:
                    scores(1 - slot, hh, kn, cols)
                out += consume(slot, hh, j, carry[2 * hh], carry[2 * hh + 1])
            return tuple(out)

        def pair(jj, carry, step=step, first=first):
            return step(2 * jj + 1, 1 - first, step(2 * jj, first, carry))

        regular = max(qi - 1, 0)
        state = lax.fori_loop(0, regular // 2, pair, tuple(state))
        if regular % 2:
            state = step(regular - 1, first, state)
        if qi >= 1:
            state = step(qi - 1, (first + qi - 1) % 2, state, last=True)
        diag_slot = (first + qi) % 2
        pending = (qi, diag_slot, state)
        first = 1 - diag_slot
    finish(*pending)


def _mlp_kernel(x_ref, yr_ref, ot_ref, p_ref, gatt_ref, wo1_ref, wo2_ref, gffn_ref, wg_ref,
                wu_ref, wd_ref, gpin_ref, wpg_ref, wpp_ref, gpost_ref, gfin_ref, out_ref,
                acc_ref, vff_ref):
    ya = _rms(ot_ref[...].T, gatt_ref[...]).astype(BF16)
    h = x_ref[...] + _dot(yr_ref[...], wo1_ref[...]) + _dot(ya, wo2_ref[...])
    acc_ref[...] = h
    vff_ref[...] = _rms(h, gffn_ref[...]).astype(BF16)

    def chunk(c, carry):
        cols = pl.ds(pl.multiple_of(c * FF_CHUNK, FF_CHUNK), FF_CHUNK)
        g = _dot(vff_ref[...], wg_ref[:, cols])
        u = _dot(vff_ref[...], wu_ref[:, cols])
        act = (jax.nn.silu(g) * u).astype(BF16)
        acc_ref[...] += _dot(act, wd_ref[cols, :])
        return carry

    lax.fori_loop(0, wg_ref.shape[1] // FF_CHUNK, chunk, 0, unroll=FF_UNROLL)
    h = acc_ref[...]
    e = _rms(_dot(p_ref[...].astype(BF16), wpp_ref[...]), gpost_ref[...])
    gate = jax.nn.sigmoid(_dot(_rms(h, gpin_ref[...]).astype(BF16), wpg_ref[...]))
    out_ref[...] = _rms(h + gate * e, gfin_ref[...])


def _const_spec(shape):
    zeros = (0,) * len(shape)
    return pl.BlockSpec(shape, lambda *_: zeros)


def _tiles(seq):
    proj_t = min(1024, seq)
    att_t = min(512, seq)
    return proj_t, min(RNN_TIME_TILE, seq), att_t, min(512, seq)


def _layer(h, p_l, pos, inv, prm):
    b, s, _ = h.shape
    proj_t, rnn_t, att_t, mlp_t = _tiles(s)
    n = b * s
    cparams = functools.partial(pltpu.CompilerParams, vmem_limit_bytes=VMEM_LIMIT)

    qt, k, vt = pl.pallas_call(
        functools.partial(_proj_kernel, att_t=att_t),
        grid=(b, s // proj_t),
        in_specs=[
            pl.BlockSpec((None, proj_t, D_MODEL), lambda i, j: (i, j, 0)),
            pl.BlockSpec((None, 1, proj_t), lambda i, j: (i, 0, j)),
            _const_spec((QK_ROPE // 2, 1)),
            _const_spec((1, D_MODEL)),
            _const_spec((D_MODEL, D_IN_ATT)),
            _const_spec((1, Q_LORA)),
            _const_spec((MLA_HEADS * HEAD_PAD, Q_LORA)),
            _const_spec((1, KV_LORA)),
            _const_spec((2 * LANES, MLA_HEADS * HEAD_PAD)),
            _const_spec((D_ATT, KV_LORA)),
        ],
        out_specs=[
            pl.BlockSpec((None, MLA_HEADS * HEAD_PAD, proj_t), lambda i, j: (i, 0, j)),
            pl.BlockSpec((None, proj_t, MLA_HEADS * HEAD_PAD), lambda i, j: (i, j, 0)),
            pl.BlockSpec((None, proj_t // att_t, D_ATT, att_t), lambda i, j: (i, j, 0, 0)),
        ],
        out_shape=[
            jax.ShapeDtypeStruct((b, MLA_HEADS * HEAD_PAD, s), BF16),
            jax.ShapeDtypeStruct((b, s, MLA_HEADS * HEAD_PAD), BF16),
            jax.ShapeDtypeStruct((b, s // att_t, D_ATT, att_t), BF16),
        ],
        compiler_params=cparams(dimension_semantics=("arbitrary", "arbitrary")),
        name="proj",
    )(h, pos, inv, prm["g_mix"], prm["w_in_att"], prm["g_q"], prm["wq_t"], prm["g_kv"],
      prm["w_k"], prm["wv_t"])

    rnn_rows = rnn_t * b
    sub_t = min(RNN_SUB_STEPS, rnn_t)
    sub_rows = sub_t * b
    tb = jnp.arange(sub_rows)
    perm = ((tb % b) * sub_t + tb // b)[:, None] == tb[None, :]
    perm = perm.astype(BF16)
    y_rnn = pl.pallas_call(
        _rglru_kernel,
        grid=(s // rnn_t,),
        in_specs=[
            pl.BlockSpec((b, rnn_t, D_MODEL), lambda i: (0, i, 0)),
            _const_spec((1, D_MODEL)),
            _const_spec((D_MODEL, 2 * D_RNN)),
            _const_spec((sub_rows, sub_rows)),
            _const_spec((sub_rows, sub_rows)),
            _const_spec((CONV_W, D_RNN)),
            _const_spec((1, D_RNN)),
            _const_spec((D_RNN // LANES, LANES, 2 * LANES)),
            _const_spec((1, D_RNN)),
            _const_spec((1, D_RNN)),
            _const_spec((1, D_RNN)),
            _const_spec((1, D_RNN)),
        ],
        out_specs=pl.BlockSpec((b, rnn_t, D_RNN), lambda i: (0, i, 0)),
        out_shape=jax.ShapeDtypeStruct((b, s, D_RNN), BF16),
        scratch_shapes=[
            pltpu.VMEM((rnn_rows + (CONV_W - 1) * b, D_RNN), F32),
            pltpu.VMEM((rnn_rows, D_RNN), F32),
            pltpu.VMEM((sub_rows, D_RNN), F32),
            pltpu.VMEM((sub_rows, D_RNN), F32),
            pltpu.VMEM((b, D_RNN), F32),
        ],
        compiler_params=cparams(dimension_semantics=("arbitrary",)),
        name="rglru",
    )(h, prm["g_mix"], prm["w_in_rnn"], perm, perm.T, prm["conv_w"], prm["conv_b"],
      prm["w_gates"], prm["b_a"], prm["b_x"], prm["lru_l"], prm["g_out_rnn"])

    hg = ATT_HEADS_PER_STEP
    o_t = pl.pallas_call(
        functools.partial(_attn_kernel, att_t=att_t, heads=hg),
        grid=(b, MLA_HEADS // hg),
        in_specs=[
            pl.BlockSpec((None, hg * HEAD_PAD, s), lambda i, j: (i, j, 0)),
            pl.BlockSpec((None, s, hg * HEAD_PAD), lambda i, j: (i, 0, j)),
            pl.BlockSpec((None, s // att_t, hg * V_DIM, att_t), lambda i, j: (i, 0, j, 0)),
        ],
        out_specs=pl.BlockSpec((None, hg * V_DIM, s), lambda i, j: (i, j, 0)),
        out_shape=jax.ShapeDtypeStruct((b, D_ATT, s), F32),
        scratch_shapes=[pltpu.VMEM((2, hg, att_t, att_t), F32)],
        compiler_params=cparams(dimension_semantics=("arbitrary", "arbitrary")),
        name="attn",
    )(qt, k, vt)

    spt = s // mlp_t
    out = pl.pallas_call(
        _mlp_kernel,
        grid=(n // mlp_t,),
        in_specs=[
            pl.BlockSpec((mlp_t, D_MODEL), lambda i: (i, 0)),
            pl.BlockSpec((mlp_t, D_RNN), lambda i: (i, 0)),
            pl.BlockSpec((None, D_ATT, mlp_t), lambda i: (i // spt, 0, i % spt)),
            pl.BlockSpec((mlp_t, PLE_DIM), lambda i: (i, 0)),
            _const_spec((1, D_ATT)),
            _const_spec((D_RNN, D_MODEL)),
            _const_spec((D_ATT, D_MODEL)),
            _const_spec((1, D_MODEL)),
            _const_spec((D_MODEL, D_FF)),
            _const_spec((D_MODEL, D_FF)),
            _const_spec((D_FF, D_MODEL)),
            _const_spec((1, D_MODEL)),
            _const_spec((D_MODEL, D_MODEL)),
            _const_spec((PLE_DIM, D_MODEL)),
            _const_spec((1, D_MODEL)),
            _const_spec((1, D_MODEL)),
        ],
        out_specs=pl.BlockSpec((mlp_t, D_MODEL), lambda i: (i, 0)),
        out_shape=jax.ShapeDtypeStruct((n, D_MODEL), F32),
        scratch_shapes=[
            pltpu.VMEM((mlp_t, D_MODEL), F32),
            pltpu.VMEM((mlp_t, D_MODEL), BF16),
        ],
        compiler_params=cparams(dimension_semantics=("arbitrary",)),
        name="mlp",
    )(h.reshape(n, D_MODEL), y_rnn.reshape(n, D_RNN), o_t, p_l.reshape(n, PLE_DIM),
      prm["g_out_att"], prm["w_out_rnn"], prm["w_out_att"], prm["g_ffn"], prm["w_gate"],
      prm["w_up"], prm["w_down"], prm["g_ple_in"], prm["w_ple_gate"], prm["w_ple_proj"], prm["g_ple_post"],
      prm["g_final"])
    return out.reshape(b, s, D_MODEL)


def _prep_layer(l, g_mix, w_in, conv_w, conv_b, w_rg_a, b_rg_a, w_rg_x, b_rg_x, lru_L, g_q_lat,
                w_q_up, g_kv_lat, w_kv_up, g_out_rnn, g_out_att, w_out, g_ffn, w_ffn_gate,
                w_ffn_up, w_ffn_down, g_ple_in, w_ple_gate, w_ple_proj, g_ple_post, g_final):
    row = lambda v: v.reshape(1, -1).astype(F32)
    half = QK_ROPE // 2
    w = w_in[l]
    w_kr = w[:, -QK_ROPE:]
    w_kr_rot = jnp.concatenate([-w_kr[:, half:], w_kr[:, :half]], axis=1)
    w_att = w[:, 2 * D_RNN:]
    w_in_att = jnp.concatenate(
        [w_att, w_kr_rot, jnp.zeros((D_MODEL, D_IN_ATT - w_att.shape[1] - QK_ROPE), F32)], axis=1)

    wq = w_q_up[l].reshape(Q_LORA, MLA_HEADS, QK_DIM)
    wq = jnp.pad(wq, ((0, 0), (0, 0), (0, HEAD_PAD - QK_DIM)))
    wq_t = wq.reshape(Q_LORA, MLA_HEADS * HEAD_PAD).T

    wkv = w_kv_up[l].reshape(KV_LORA, MLA_HEADS, QK_NOPE + V_DIM)
    wk_nope = jnp.pad(wkv[:, :, :QK_NOPE], ((0, 0), (0, 0), (0, HEAD_PAD - QK_NOPE)))
    wk_nope = wk_nope.reshape(KV_LORA, MLA_HEADS * HEAD_PAD)
    place = jnp.pad(jnp.eye(QK_ROPE, dtype=F32), ((0, LANES - QK_ROPE), (QK_NOPE, HEAD_PAD - QK_DIM)))
    w_k = jnp.concatenate([wk_nope, jnp.tile(place, (1, MLA_HEADS))], axis=0)
    wv_t = wkv[:, :, QK_NOPE:].reshape(KV_LORA, D_ATT).T

    def blockdiag(wb):
        wb = wb.reshape(D_RNN // LANES, 2, RNN_BW, RNN_BW)
        z = jnp.zeros_like(wb[:, 0])
        top = jnp.concatenate([wb[:, 0], z], axis=2)
        bot = jnp.concatenate([z, wb[:, 1]], axis=2)
        return jnp.concatenate([top, bot], axis=1)

    w_gates = jnp.concatenate([blockdiag(w_rg_a[l]), blockdiag(w_rg_x[l])], axis=2)

    return dict(
        g_mix=row(g_mix[l]), w_in_att=w_in_att.astype(BF16),
        w_in_rnn=w[:, :2 * D_RNN].astype(BF16), g_q=row(g_q_lat[l]),
        wq_t=wq_t.astype(BF16), g_kv=row(g_kv_lat[l]), w_k=w_k.astype(BF16),
        wv_t=wv_t.astype(BF16),
        conv_w=conv_w[l].astype(F32), conv_b=row(conv_b[l]), w_gates=w_gates.astype(BF16),
        b_a=row(b_rg_a[l]), b_x=row(b_rg_x[l]), lru_l=row(lru_L[l]), g_out_rnn=row(g_out_rnn[l]),
        g_out_att=row(g_out_att[l]), w_out_rnn=w_out[l][:D_RNN].astype(BF16),
        w_out_att=w_out[l][D_RNN:].astype(BF16), g_ffn=row(g_ffn[l]),
        w_gate=w_ffn_gate[l].astype(BF16), w_up=w_ffn_up[l].astype(BF16),
        w_down=w_ffn_down[l].astype(BF16),
        g_ple_in=row(g_ple_in[l]), w_ple_gate=w_ple_gate[l].astype(BF16),
        w_ple_proj=w_ple_proj[l].astype(BF16), g_ple_post=row(g_ple_post[l]),
        g_final=row(g_final),
    )


def kernel(x, p, positions, g_mix, w_in, conv_w, conv_b, w_rg_a, b_rg_a, w_rg_x, b_rg_x, lru_L, g_q_lat, w_q_up, g_kv_lat, w_kv_up, g_out_rnn, g_out_att, w_out, g_ffn, w_ffn_gate, w_ffn_up, w_ffn_down, g_ple_in, w_ple_gate, w_ple_proj, g_ple_post, g_final):
    b, s = positions.shape
    pos = positions.reshape(b, 1, s)
    inv = (ROPE_BASE ** (-jnp.arange(0, QK_ROPE, 2, dtype=F32) / QK_ROPE)).reshape(-1, 1)
    h = x
    depth = p.shape[0]
    assert depth == 1, "the mlp kernel applies the final norm, so only a single layer is supported"
    for l in range(depth):
        prm = _prep_layer(l, g_mix, w_in, conv_w, conv_b, w_rg_a, b_rg_a, w_rg_x, b_rg_x, lru_L,
                          g_q_lat, w_q_up, g_kv_lat, w_kv_up, g_out_rnn, g_out_att, w_out, g_ffn,
                          w_ffn_gate, w_ffn_up, w_ffn_down, g_ple_in, w_ple_gate, w_ple_proj,
                          g_ple_post, g_final)
        h = _layer(h, p[l], pos, inv, prm)
    return h
```

```python
import functools

import jax
import jax.numpy as jnp
from jax import lax
from jax.experimental import pallas as pl
from jax.experimental.pallas import tpu as pltpu

F32 = jnp.float32
BF16 = jnp.bfloat16

EPS = 1e-6
D_MODEL = 1024
D_RNN = 512
RNN_BLOCKS = 8
RNN_BW = 64
CONV_W = 4
LRU_C = 8.0
MLA_HEADS = 8
QK_NOPE = 64
QK_ROPE = 32
V_DIM = 64
Q_LORA = 256
KV_LORA = 128
D_ATT = MLA_HEADS * V_DIM
QK_DIM = QK_NOPE + QK_ROPE
ATT_SCALE = QK_DIM ** -0.5
ROPE_BASE = 10000.0
D_FF = 2816
PLE_DIM = 256

LANES = 128
SUBLANES = 8
HEAD_PAD = LANES
D_IN_ATT = 512
PROJ_SUB_TILE = 512
RNN_TIME_TILE = 128
RNN_SUB_STEPS = 16
FF_CHUNK = 256
FF_UNROLL = 4
VMEM_LIMIT = 56 * 1024 * 1024
ATT_HEADS_PER_STEP = 4
SUM_ROWS = 16
LOG2E = 1.4426950408889634
Q_SCALE = ATT_SCALE * LOG2E


def _rms(x, g):
    ms = jnp.mean(x * x, axis=-1, keepdims=True)
    return x * lax.rsqrt(ms + EPS) * g


def _dot(a, b):
    return jnp.dot(a, b, preferred_element_type=F32)


def _dot_nt(a, b):
    return lax.dot_general(a, b, (((1,), (1,)), ((), ())), preferred_element_type=F32)


def _proj_kernel(x_ref, pos_ref, inv_ref, gmix_ref, win_ref, gq_ref, wqt_ref, gkv_ref,
                 wk_ref, wvt_ref, qt_ref, k_ref, vt_ref, *, att_t):
    half = QK_ROPE // 2
    sub = min(PROJ_SUB_TILE, att_t)
    for i in range(x_ref.shape[0] // sub):
        rows = slice(i * sub, (i + 1) * sub)
        ang = inv_ref[...] * pos_ref[:, rows].astype(F32)
        sin_t, cos_t = jnp.sin(ang), jnp.cos(ang)
        u = _rms(x_ref[rows, :], gmix_ref[...]).astype(BF16)
        z = _dot(u, win_ref[...])
        c_q = z[:, :Q_LORA]
        c_kv = z[:, Q_LORA:Q_LORA + KV_LORA]
        zk = z[:, Q_LORA + KV_LORA:]

        cqn = _rms(c_q, gq_ref[...]).astype(BF16)
        qt = _dot_nt(wqt_ref[...], cqn)
        for h in range(MLA_HEADS):
            base = h * HEAD_PAD
            r0 = base + QK_NOPE
            qt_ref[base:r0, rows] = (qt[base:r0] * Q_SCALE).astype(BF16)
            t1 = qt[r0:r0 + half]
            t2 = qt[r0 + half:r0 + QK_ROPE]
            qt_ref[r0:r0 + half, rows] = ((t1 * cos_t - t2 * sin_t) * Q_SCALE).astype(BF16)
            qt_ref[r0 + half:r0 + QK_ROPE, rows] = ((t1 * sin_t + t2 * cos_t) * Q_SCALE).astype(BF16)
            qt_ref[r0 + QK_ROPE:base + HEAD_PAD, rows] = jnp.zeros((HEAD_PAD - QK_DIM, sub), BF16)

        ckvn = _rms(c_kv, gkv_ref[...]).astype(BF16)
        tabk = jnp.concatenate(
            [cos_t, cos_t, sin_t, sin_t, jnp.zeros((LANES - 2 * QK_ROPE, sub), F32)], axis=0).T
        prod = zk * tabk
        kr = prod + pltpu.roll(prod, LANES - QK_ROPE, 1)
        kcat = jnp.concatenate([ckvn, kr.astype(BF16)], axis=1)
        k_ref[rows, :] = _dot(kcat, wk_ref[...]).astype(BF16)
        vt = _dot_nt(wvt_ref[...], ckvn).astype(BF16)
        start = i * sub
        vt_ref[start // att_t, :, start % att_t:start % att_t + sub] = vt


def _sigmoid(x):
    return 0.5 * jnp.tanh(0.5 * x) + 0.5


def _gelu_tanh(x):
    c0 = (2.0 / jnp.pi) ** 0.5
    hx = 0.5 * x
    return hx + hx * jnp.tanh(x * (c0 + (c0 * 0.044715) * (x * x)))


def _rglru_kernel(x_ref, gmix_ref, win_ref, perm_ref, permt_ref, convw_ref, convb_ref, wg_ref,
                  ba_ref, bx_ref, lru_ref, gout_ref, y_ref, xpad_ref, gate_ref, a_ref, b_ref, h_ref):
    nb, tt, _ = x_ref.shape
    ts = min(RNN_SUB_STEPS, tt)
    sub = nb * ts
    halo = (CONV_W - 1) * nb

    @pl.when(pl.program_id(0) == 0)
    def _():
        xpad_ref[0:halo, :] = jnp.zeros((halo, D_RNN), F32)
        h_ref[...] = jnp.zeros_like(h_ref)

    neg_l = -lru_ref[...]
    sp = jnp.maximum(neg_l, 0.0) + jnp.log1p(jnp.exp(-jnp.abs(neg_l)))
    w = convw_ref[...]

    def project(q):
        xq = x_ref[:, q * ts:(q + 1) * ts, :].reshape(sub, D_MODEL)
        u = _rms(xq, gmix_ref[...]).astype(BF16)
        zr = _dot(perm_ref[...], _dot(u, win_ref[...]).astype(BF16))
        xpad_ref[halo + q * sub:halo + (q + 1) * sub, :] = zr[:, :D_RNN]
        gate_ref[q * sub:(q + 1) * sub, :] = zr[:, D_RNN:]

    def recur(q, h):
        lo = halo + q * sub
        xc = convb_ref[...] + w[CONV_W - 1:CONV_W] * xpad_ref[lo:lo + sub, :]
        for j in range(1, CONV_W):
            xc = xc + w[CONV_W - 1 - j:CONV_W - j] * xpad_ref[lo - j * nb:lo - j * nb + sub, :]
        xcb = xc.astype(BF16)
        for g in range(D_RNN // LANES):
            sl = slice(g * LANES, (g + 1) * LANES)
            pre = _dot(xcb[:, sl], wg_ref[g])
            r = _sigmoid(pre[:, :LANES] + ba_ref[:, sl])
            i = _sigmoid(pre[:, LANES:] + bx_ref[:, sl])
            neg_log_a = (LRU_C * r) * sp[:, sl]
            a = jnp.exp2(neg_log_a * -LOG2E)
            a_ref[:, sl] = a
            qq = jnp.tanh(neg_log_a) * (1.0 + a * a)
            mult = jnp.where(qq > 0.0, qq * lax.rsqrt(qq), 0.0)
            b_ref[:, sl] = mult * (i * xc[:, sl])
        for ti in range(ts):
            at = slice(ti * nb, (ti + 1) * nb)
            h = a_ref[at, :] * h + b_ref[at, :]
            b_ref[at, :] = h
        y = _rms(b_ref[...] * _gelu_tanh(gate_ref[q * sub:(q + 1) * sub, :]), gout_ref[...])
        y = _dot(permt_ref[...], y.astype(BF16)).astype(BF16)
        y_ref[:, q * ts:(q + 1) * ts, :] = y.reshape(nb, ts, D_RNN)
        return h

    n_sub = tt // ts
    h = h_ref[...]
    project(0)
    for q in range(n_sub):
        if q + 1 < n_sub:
            project(q + 1)
        h = recur(q, h)
    h_ref[...] = h
    xpad_ref[0:halo, :] = xpad_ref[nb * tt:, :]


def _attn_kernel(qt_ref, k_ref, vt_ref, o_ref, s_ref, *, att_t, heads):
    s_len = k_ref.shape[0]
    t = att_t
    h2 = t // 2
    kidx = lax.broadcasted_iota(jnp.int32, (t, t), 0)
    qidx = lax.broadcasted_iota(jnp.int32, (t, t), 1)
    causal = kidx <= qidx
    ones = jnp.ones((SUM_ROWS, t), BF16)

    def head_rows(hh, width):
        return slice(hh * width, (hh + 1) * width)

    def scores(slot, hh, kj, cols):
        qt = qt_ref[head_rows(hh, HEAD_PAD), cols]
        s_ref[slot, hh] = _dot(kj[:, head_rows(hh, HEAD_PAD)], qt)

    def diag_scores(slot, hh, qi):
        lo = qi * t
        qt = qt_ref[head_rows(hh, HEAD_PAD), lo:lo + t]
        s_ref[slot, hh, 0:h2, :] = _dot(k_ref[lo:lo + h2, head_rows(hh, HEAD_PAD)], qt)
        s_ref[slot, hh, h2:t, h2:t] = _dot(k_ref[lo + h2:lo + t, head_rows(hh, HEAD_PAD)], qt[:, h2:])

    def softmax_step(s, m, acc, vj):
        m_new = jnp.maximum(m, jnp.max(s, axis=0, keepdims=True))
        p = jnp.exp2(s - m_new).astype(BF16)
        return m_new, jnp.exp2(m - m_new) * acc + _dot(vj, p)

    def values(hh, j):
        return jnp.concatenate([vt_ref[j, head_rows(hh, V_DIM), :], ones], axis=0)

    def consume(slot, hh, j, m, acc):
        return softmax_step(s_ref[slot, hh], m, acc, values(hh, j))

    def finish(qi, slot, state):
        for hh in range(heads):
            m, acc = state[2 * hh], state[2 * hh + 1]
            vj = values(hh, qi)
            halves = (
                (slice(0, h2), slice(0, h2)),
                (slice(0, t), slice(h2, t)),
            )
            for keys, qcols in halves:
                s = jnp.where(causal[keys, qcols], s_ref[slot, hh, keys, qcols], -jnp.inf)
                _, a = softmax_step(s, m[:, qcols], acc[:, qcols], vj[:, keys])
                o_ref[head_rows(hh, V_DIM), qi * t + qcols.start:qi * t + qcols.stop] = (
                    a[:V_DIM] / a[V_DIM:V_DIM + 1])

    first = 0
    pending = None
    for qi in range(s_len // t):
        cols = slice(qi * t, (qi + 1) * t)
        k0 = k_ref[0:t, :]
        state = []
        for hh in range(heads):
            if qi == 0:
                diag_scores(first, hh, qi)
            else:
                scores(first, hh, k0, cols)
            state += [jnp.full((1, t), -jnp.inf, F32), jnp.zeros((V_DIM + SUM_ROWS, t), F32)]
        if pending is not None:
            finish(*pending)

        def step(j, slot, carry, last=False, cols=cols, qi=qi):
            if not last:
                kn = k_ref[pl.ds(pl.multiple_of((j + 1) * t, t), t), :]
            out = []
            for hh in range(heads):
                if last:
                    diag_scores(1 - slot, hh, qi)
                else:
                    scores(1 - slot, hh, kn, cols)
                out += consume(slot, hh, j, carry[2 * hh], carry[2 * hh + 1])
            return tuple(out)

        def pair(jj, carry, step=step, first=first):
            return step(2 * jj + 1, 1 - first, step(2 * jj, first, carry))

        regular = max(qi - 1, 0)
        state = lax.fori_loop(0, regular // 2, pair, tuple(state))
        if regular % 2:
            state = step(regular - 1, first, state)
        if qi >= 1:
            state = step(qi - 1, (first + qi - 1) % 2, state, last=True)
        diag_slot = (first + qi) % 2
        pending = (qi, diag_slot, state)
        first = 1 - diag_slot
    finish(*pending)


def _mlp_kernel(x_ref, yr_ref, ot_ref, p_ref, gatt_ref, wo1_ref, wo2_ref, gffn_ref, wg_ref,
                wu_ref, wd_ref, gpin_ref, wpg_ref, wpp_ref, gpost_ref, gfin_ref, out_ref,
                acc_ref, vff_ref):
    ya = _rms(ot_ref[...].T, gatt_ref[...]).astype(BF16)
    h = x_ref[...] + _dot(yr_ref[...], wo1_ref[...]) + _dot(ya, wo2_ref[...])
    acc_ref[...] = h
    vff_ref[...] = _rms(h, gffn_ref[...]).astype(BF16)

    def chunk(c, carry):
        cols = pl.ds(pl.multiple_of(c * FF_CHUNK, FF_CHUNK), FF_CHUNK)
        g = _dot(vff_ref[...], wg_ref[:, cols])
        u = _dot(vff_ref[...], wu_ref[:, cols])
        act = (jax.nn.silu(g) * u).astype(BF16)
        acc_ref[...] += _dot(act, wd_ref[cols, :])
        return carry

    lax.fori_loop(0, wg_ref.shape[1] // FF_CHUNK, chunk, 0, unroll=FF_UNROLL)
    h = acc_ref[...]
    e = _rms(_dot(p_ref[...].astype(BF16), wpp_ref[...]), gpost_ref[...])
    gate = jax.nn.sigmoid(_dot(_rms(h, gpin_ref[...]).astype(BF16), wpg_ref[...]))
    out_ref[...] = _rms(h + gate * e, gfin_ref[...])


def _const_spec(shape):
    zeros = (0,) * len(shape)
    return pl.BlockSpec(shape, lambda *_: zeros)


def _tiles(seq):
    proj_t = min(1024, seq)
    att_t = min(512, seq)
    return proj_t, min(RNN_TIME_TILE, seq), att_t, min(512, seq)


def _layer(h, p_l, pos, inv, prm):
    b, s, _ = h.shape
    proj_t, rnn_t, att_t, mlp_t = _tiles(s)
    n = b * s
    cparams = functools.partial(pltpu.CompilerParams, vmem_limit_bytes=VMEM_LIMIT)

    qt, k, vt = pl.pallas_call(
        functools.partial(_proj_kernel, att_t=att_t),
        grid=(b, s // proj_t),
        in_specs=[
            pl.BlockSpec((None, proj_t, D_MODEL), lambda i, j: (i, j, 0)),
            pl.BlockSpec((None, 1, proj_t), lambda i, j: (i, 0, j)),
            _const_spec((QK_ROPE // 2, 1)),
            _const_spec((1, D_MODEL)),
            _const_spec((D_MODEL, D_IN_ATT)),
            _const_spec((1, Q_LORA)),
            _const_spec((MLA_HEADS * HEAD_PAD, Q_LORA)),
            _const_spec((1, KV_LORA)),
            _const_spec((2 * LANES, MLA_HEADS * HEAD_PAD)),
            _const_spec((D_ATT, KV_LORA)),
        ],
        out_specs=[
            pl.BlockSpec((None, MLA_HEADS * HEAD_PAD, proj_t), lambda i, j: (i, 0, j)),
            pl.BlockSpec((None, proj_t, MLA_HEADS * HEAD_PAD), lambda i, j: (i, j, 0)),
            pl.BlockSpec((None, proj_t // att_t, D_ATT, att_t), lambda i, j: (i, j, 0, 0)),
        ],
        out_shape=[
            jax.ShapeDtypeStruct((b, MLA_HEADS * HEAD_PAD, s), BF16),
            jax.ShapeDtypeStruct((b, s, MLA_HEADS * HEAD_PAD), BF16),
            jax.ShapeDtypeStruct((b, s // att_t, D_ATT, att_t), BF16),
        ],
        compiler_params=cparams(dimension_semantics=("arbitrary", "arbitrary")),
        name="proj",
    )(h, pos, inv, prm["g_mix"], prm["w_in_att"], prm["g_q"], prm["wq_t"], prm["g_kv"],
      prm["w_k"], prm["wv_t"])

    rnn_rows = rnn_t * b
    sub_t = min(RNN_SUB_STEPS, rnn_t)
    sub_rows = sub_t * b
    tb = jnp.arange(sub_rows)
    perm = ((tb % b) * sub_t + tb // b)[:, None] == tb[None, :]
    perm = perm.astype(BF16)
    y_rnn = pl.pallas_call(
        _rglru_kernel,
        grid=(s // rnn_t,),
        in_specs=[
            pl.BlockSpec((b, rnn_t, D_MODEL), lambda i: (0, i, 0)),
            _const_spec((1, D_MODEL)),
            _const_spec((D_MODEL, 2 * D_RNN)),
            _const_spec((sub_rows, sub_rows)),
            _const_spec((sub_rows, sub_rows)),
            _const_spec((CONV_W, D_RNN)),
            _const_spec((1, D_RNN)),
            _const_spec((D_RNN // LANES, LANES, 2 * LANES)),
            _const_spec((1, D_RNN)),
            _const_spec((1, D_RNN)),
            _const_spec((1, D_RNN)),
            _const_spec((1, D_RNN)),
        ],
        out_specs=pl.BlockSpec((b, rnn_t, D_RNN), lambda i: (0, i, 0)),
        out_shape=jax.ShapeDtypeStruct((b, s, D_RNN), BF16),
        scratch_shapes=[
            pltpu.VMEM((rnn_rows + (CONV_W - 1) * b, D_RNN), F32),
            pltpu.VMEM((rnn_rows, D_RNN), F32),
            pltpu.VMEM((sub_rows, D_RNN), F32),
            pltpu.VMEM((sub_rows, D_RNN), F32),
            pltpu.VMEM((b, D_RNN), F32),
        ],
        compiler_params=cparams(dimension_semantics=("arbitrary",)),
        name="rglru",
    )(h, prm["g_mix"], prm["w_in_rnn"], perm, perm.T, prm["conv_w"], prm["conv_b"],
      prm["w_gates"], prm["b_a"], prm["b_x"], prm["lru_l"], prm["g_out_rnn"])

    hg = ATT_HEADS_PER_STEP
    o_t = pl.pallas_call(
        functools.partial(_attn_kernel, att_t=att_t, heads=hg),
        grid=(b, MLA_HEADS // hg),
        in_specs=[
            pl.BlockSpec((None, hg * HEAD_PAD, s), lambda i, j: (i, j, 0)),
            pl.BlockSpec((None, s, hg * HEAD_PAD), lambda i, j: (i, 0, j)),
            pl.BlockSpec((None, s // att_t, hg * V_DIM, att_t), lambda i, j: (i, 0, j, 0)),
        ],
        out_specs=pl.BlockSpec((None, hg * V_DIM, s), lambda i, j: (i, j, 0)),
        out_shape=jax.ShapeDtypeStruct((b, D_ATT, s), F32),
        scratch_shapes=[pltpu.VMEM((2, hg, att_t, att_t), F32)],
        compiler_params=cparams(dimension_semantics=("arbitrary", "arbitrary")),
        name="attn",
    )(qt, k, vt)

    spt = s // mlp_t
    out = pl.pallas_call(
        _mlp_kernel,
        grid=(n // mlp_t,),
        in_specs=[
            pl.BlockSpec((mlp_t, D_MODEL), lambda i: (i, 0)),
            pl.BlockSpec((mlp_t, D_RNN), lambda i: (i, 0)),
            pl.BlockSpec((None, D_ATT, mlp_t), lambda i: (i // spt, 0, i % spt)),
            pl.BlockSpec((mlp_t, PLE_DIM), lambda i: (i, 0)),
            _const_spec((1, D_ATT)),
            _const_spec((D_RNN, D_MODEL)),
            _const_spec((D_ATT, D_MODEL)),
            _const_spec((1, D_MODEL)),
            _const_spec((D_MODEL, D_FF)),
            _const_spec((D_MODEL, D_FF)),
            _const_spec((D_FF, D_MODEL)),
            _const_spec((1, D_MODEL)),
            _const_spec((D_MODEL, D_MODEL)),
            _const_spec((PLE_DIM, D_MODEL)),
            _const_spec((1, D_MODEL)),
            _const_spec((1, D_MODEL)),
        ],
        out_specs=pl.BlockSpec((mlp_t, D_MODEL), lambda i: (i, 0)),
        out_shape=jax.ShapeDtypeStruct((n, D_MODEL), F32),
        scratch_shapes=[
            pltpu.VMEM((mlp_t, D_MODEL), F32),
            pltpu.VMEM((mlp_t, D_MODEL), BF16),
        ],
        compiler_params=cparams(dimension_semantics=("arbitrary",)),
        name="mlp",
    )(h.reshape(n, D_MODEL), y_rnn.reshape(n, D_RNN), o_t, p_l.reshape(n, PLE_DIM),
      prm["g_out_att"], prm["w_out_rnn"], prm["w_out_att"], prm["g_ffn"], prm["w_gate"],
      prm["w_up"], prm["w_down"], prm["g_ple_in"], prm["w_ple_gate"], prm["w_ple_proj"], prm["g_ple_post"],
      prm["g_final"])
    return out.reshape(b, s, D_MODEL)


def _prep_layer(l, g_mix, w_in, conv_w, conv_b, w_rg_a, b_rg_a, w_rg_x, b_rg_x, lru_L, g_q_lat,
                w_q_up, g_kv_lat, w_kv_up, g_out_rnn, g_out_att, w_out, g_ffn, w_ffn_gate,
                w_ffn_up, w_ffn_down, g_ple_in, w_ple_gate, w_ple_proj, g_ple_post, g_final):
    row = lambda v: v.reshape(1, -1).astype(F32)
    half = QK_ROPE // 2
    w = w_in[l]
    w_kr = w[:, -QK_ROPE:]
    w_kr_rot = jnp.concatenate([-w_kr[:, half:], w_kr[:, :half]], axis=1)
    w_att = w[:, 2 * D_RNN:]
    w_in_att = jnp.concatenate(
        [w_att, w_kr_rot, jnp.zeros((D_MODEL, D_IN_ATT - w_att.shape[1] - QK_ROPE), F32)], axis=1)

    wq = w_q_up[l].reshape(Q_LORA, MLA_HEADS, QK_DIM)
    wq = jnp.pad(wq, ((0, 0), (0, 0), (0, HEAD_PAD - QK_DIM)))
    wq_t = wq.reshape(Q_LORA, MLA_HEADS * HEAD_PAD).T

    wkv = w_kv_up[l].reshape(KV_LORA, MLA_HEADS, QK_NOPE + V_DIM)
    wk_nope = jnp.pad(wkv[:, :, :QK_NOPE], ((0, 0), (0, 0), (0, HEAD_PAD - QK_NOPE)))
    wk_nope = wk_nope.reshape(KV_LORA, MLA_HEADS * HEAD_PAD)
    place = jnp.pad(jnp.eye(QK_ROPE, dtype=F32), ((0, LANES - QK_ROPE), (QK_NOPE, HEAD_PAD - QK_DIM)))
    w_k = jnp.concatenate([wk_nope, jnp.tile(place, (1, MLA_HEADS))], axis=0)
    wv_t = wkv[:, :, QK_NOPE:].reshape(KV_LORA, D_ATT).T

    def blockdiag(wb):
        wb = wb.reshape(D_RNN // LANES, 2, RNN_BW, RNN_BW)
        z = jnp.zeros_like(wb[:, 0])
        top = jnp.concatenate([wb[:, 0], z], axis=2)
        bot = jnp.concatenate([z, wb[:, 1]], axis=2)
        return jnp.concatenate([top, bot], axis=1)

    w_gates = jnp.concatenate([blockdiag(w_rg_a[l]), blockdiag(w_rg_x[l])], axis=2)

    return dict(
        g_mix=row(g_mix[l]), w_in_att=w_in_att.astype(BF16),
        w_in_rnn=w[:, :2 * D_RNN].astype(BF16), g_q=row(g_q_lat[l]),
        wq_t=wq_t.astype(BF16), g_kv=row(g_kv_lat[l]), w_k=w_k.astype(BF16),
        wv_t=wv_t.astype(BF16),
        conv_w=conv_w[l].astype(F32), conv_b=row(conv_b[l]), w_gates=w_gates.astype(BF16),
        b_a=row(b_rg_a[l]), b_x=row(b_rg_x[l]), lru_l=row(lru_L[l]), g_out_rnn=row(g_out_rnn[l]),
        g_out_att=row(g_out_att[l]), w_out_rnn=w_out[l][:D_RNN].astype(BF16),
        w_out_att=w_out[l][D_RNN:].astype(BF16), g_ffn=row(g_ffn[l]),
        w_gate=w_ffn_gate[l].astype(BF16), w_up=w_ffn_up[l].astype(BF16),
        w_down=w_ffn_down[l].astype(BF16),
        g_ple_in=row(g_ple_in[l]), w_ple_gate=w_ple_gate[l].astype(BF16),
        w_ple_proj=w_ple_proj[l].astype(BF16), g_ple_post=row(g_ple_post[l]),
        g_final=row(g_final),
    )


def kernel(x, p, positions, g_mix, w_in, conv_w, conv_b, w_rg_a, b_rg_a, w_rg_x, b_rg_x, lru_L, g_q_lat, w_q_up, g_kv_lat, w_kv_up, g_out_rnn, g_out_att, w_out, g_ffn, w_ffn_gate, w_ffn_up, w_ffn_down, g_ple_in, w_ple_gate, w_ple_proj, g_ple_post, g_final):
    b, s = positions.shape
    pos = positions.reshape(b, 1, s)
    inv = (ROPE_BASE ** (-jnp.arange(0, QK_ROPE, 2, dtype=F32) / QK_ROPE)).reshape(-1, 1)
    h = x
    depth = p.shape[0]
    assert depth == 1, "the mlp kernel applies the final norm, so only a single layer is supported"
    for l in range(depth):
        prm = _prep_layer(l, g_mix, w_in, conv_w, conv_b, w_rg_a, b_rg_a, w_rg_x, b_rg_x, lru_L,
                          g_q_lat, w_q_up, g_kv_lat, w_kv_up, g_out_rnn, g_out_att, w_out, g_ffn,
                          w_ffn_gate, w_ffn_up, w_ffn_down, g_ple_in, w_ple_gate, w_ple_proj,
                          g_ple_post, g_final)
        h = _layer(h, p[l], pos, inv, prm)
    return h
```

```python
import functools

import jax
import jax.numpy as jnp
from jax import lax
from jax.experimental import pallas as pl
from jax.experimental.pallas import tpu as pltpu

F32 = jnp.float32
BF16 = jnp.bfloat16

EPS = 1e-6
D_MODEL = 1024
D_RNN = 512
RNN_BLOCKS = 8
RNN_BW = 64
CONV_W = 4
LRU_C = 8.0
MLA_HEADS = 8
QK_NOPE = 64
QK_ROPE = 32
V_DIM = 64
Q_LORA = 256
KV_LORA = 128
D_ATT = MLA_HEADS * V_DIM
QK_DIM = QK_NOPE + QK_ROPE
ATT_SCALE = QK_DIM ** -0.5
ROPE_BASE = 10000.0
D_FF = 2816
PLE_DIM = 256

LANES = 128
SUBLANES = 8
HEAD_PAD = LANES
D_IN_ATT = 512
PROJ_SUB_TILE = 512
RNN_TIME_TILE = 128
RNN_SUB_STEPS = 16
FF_CHUNK = 256
FF_UNROLL = 4
VMEM_LIMIT = 56 * 1024 * 1024
ATT_HEADS_PER_STEP = 4
SUM_ROWS = 16
LOG2E = 1.4426950408889634
Q_SCALE = ATT_SCALE * LOG2E


def _rms(x, g):
    ms = jnp.mean(x * x, axis=-1, keepdims=True)
    return x * lax.rsqrt(ms + EPS) * g


def _dot(a, b):
    return jnp.dot(a, b, preferred_element_type=F32)


def _dot_nt(a, b):
    return lax.dot_general(a, b, (((1,), (1,)), ((), ())), preferred_element_type=F32)


def _proj_kernel(x_ref, pos_ref, inv_ref, gmix_ref, win_ref, gq_ref, wqt_ref, gkv_ref,
                 wk_ref, wvt_ref, qt_ref, k_ref, vt_ref, *, att_t):
    half = QK_ROPE // 2
    sub = min(PROJ_SUB_TILE, att_t)
    for i in range(x_ref.shape[0] // sub):
        rows = slice(i * sub, (i + 1) * sub)
        ang = inv_ref[...] * pos_ref[:, rows].astype(F32)
        sin_t, cos_t = jnp.sin(ang), jnp.cos(ang)
        u = _rms(x_ref[rows, :], gmix_ref[...]).astype(BF16)
        z = _dot(u, win_ref[...])
        c_q = z[:, :Q_LORA]
        c_kv = z[:, Q_LORA:Q_LORA + KV_LORA]
        zk = z[:, Q_LORA + KV_LORA:]

        cqn = _rms(c_q, gq_ref[...]).astype(BF16)
        qt = _dot_nt(wqt_ref[...], cqn)
        for h in range(MLA_HEADS):
            base = h * HEAD_PAD
            r0 = base + QK_NOPE
            qt_ref[base:r0, rows] = (qt[base:r0] * Q_SCALE).astype(BF16)
            t1 = qt[r0:r0 + half]
            t2 = qt[r0 + half:r0 + QK_ROPE]
            qt_ref[r0:r0 + half, rows] = ((t1 * cos_t - t2 * sin_t) * Q_SCALE).astype(BF16)
            qt_ref[r0 + half:r0 + QK_ROPE, rows] = ((t1 * sin_t + t2 * cos_t) * Q_SCALE).astype(BF16)
            qt_ref[r0 + QK_ROPE:base + HEAD_PAD, rows] = jnp.zeros((HEAD_PAD - QK_DIM, sub), BF16)

        ckvn = _rms(c_kv, gkv_ref[...]).astype(BF16)
        tabk = jnp.concatenate(
            [cos_t, cos_t, sin_t, sin_t, jnp.zeros((LANES - 2 * QK_ROPE, sub), F32)], axis=0).T
        prod = zk * tabk
        kr = prod + pltpu.roll(prod, LANES - QK_ROPE, 1)
        kcat = jnp.concatenate([ckvn, kr.astype(BF16)], axis=1)
        k_ref[rows, :] = _dot(kcat, wk_ref[...]).astype(BF16)
        vt = _dot_nt(wvt_ref[...], ckvn).astype(BF16)
        start = i * sub
        vt_ref[start // att_t, :, start % att_t:start % att_t + sub] = vt


def _sigmoid(x):
    return 0.5 * jnp.tanh(0.5 * x) + 0.5


def _gelu_tanh(x):
    c0 = (2.0 / jnp.pi) ** 0.5
    hx = 0.5 * x
    return hx + hx * jnp.tanh(x * (c0 + (c0 * 0.044715) * (x * x)))


def _rglru_kernel(x_ref, gmix_ref, win_ref, perm_ref, permt_ref, convw_ref, convb_ref, wg_ref,
                  ba_ref, bx_ref, lru_ref, gout_ref, y_ref, xpad_ref, gate_ref, a_ref, b_ref, h_ref):
    nb, tt, _ = x_ref.shape
    ts = min(RNN_SUB_STEPS, tt)
    sub = nb * ts
    halo = (CONV_W - 1) * nb

    @pl.when(pl.program_id(0) == 0)
    def _():
        xpad_ref[0:halo, :] = jnp.zeros((halo, D_RNN), F32)
        h_ref[...] = jnp.zeros_like(h_ref)

    neg_l = -lru_ref[...]
    sp = jnp.maximum(neg_l, 0.0) + jnp.log1p(jnp.exp(-jnp.abs(neg_l)))
    w = convw_ref[...]

    def project(q):
        xq = x_ref[:, q * ts:(q + 1) * ts, :].reshape(sub, D_MODEL)
        u = _rms(xq, gmix_ref[...]).astype(BF16)
        zr = _dot(perm_ref[...], _dot(u, win_ref[...]).astype(BF16))
        xpad_ref[halo + q * sub:halo + (q + 1) * sub, :] = zr[:, :D_RNN]
        gate_ref[q * sub:(q + 1) * sub, :] = zr[:, D_RNN:]

    def recur(q, h):
        lo = halo + q * sub
        xc = convb_ref[...] + w[CONV_W - 1:CONV_W] * xpad_ref[lo:lo + sub, :]
        for j in range(1, CONV_W):
            xc = xc + w[CONV_W - 1 - j:CONV_W - j] * xpad_ref[lo - j * nb:lo - j * nb + sub, :]
        xcb = xc.astype(BF16)
        for g in range(D_RNN // LANES):
            sl = slice(g * LANES, (g + 1) * LANES)
            pre = _dot(xcb[:, sl], wg_ref[g])
            r = _sigmoid(pre[:, :LANES] + ba_ref[:, sl])
            i = _sigmoid(pre[:, LANES:] + bx_ref[:, sl])
            neg_log_a = (LRU_C * r) * sp[:, sl]
            a = jnp.exp2(neg_log_a * -LOG2E)
            a_ref[:, sl] = a
            qq = jnp.tanh(neg_log_a) * (1.0 + a * a)
            mult = jnp.where(qq > 0.0, qq * lax.rsqrt(qq), 0.0)
            b_ref[:, sl] = mult * (i * xc[:, sl])
        for ti in range(ts):
            at = slice(ti * nb, (ti + 1) * nb)
            h = a_ref[at, :] * h + b_ref[at, :]
            b_ref[at, :] = h
        y = _rms(b_ref[...] * _gelu_tanh(gate_ref[q * sub:(q + 1) * sub, :]), gout_ref[...])
        y = _dot(permt_ref[...], y.astype(BF16)).astype(BF16)
        y_ref[:, q * ts:(q + 1) * ts, :] = y.reshape(nb, ts, D_RNN)
        return h

    n_sub = tt // ts
    h = h_ref[...]
    project(0)
    for q in range(n_sub):
        if q + 1 < n_sub:
            project(q + 1)
        h = recur(q, h)
    h_ref[...] = h
    xpad_ref[0:halo, :] = xpad_ref[nb * tt:, :]


def _attn_kernel(qt_ref, k_ref, vt_ref, o_ref, s_ref, *, att_t, heads):
    s_len = k_ref.shape[0]
    t = att_t
    h2 = t // 2
    kidx = lax.broadcasted_iota(jnp.int32, (t, t), 0)
    qidx = lax.broadcasted_iota(jnp.int32, (t, t), 1)
    causal = kidx <= qidx
    ones = jnp.ones((SUM_ROWS, t), BF16)

    def head_rows(hh, width):
        return slice(hh * width, (hh + 1) * width)

    def scores(slot, hh, kj, cols):
        qt = qt_ref[head_rows(hh, HEAD_PAD), cols]
        s_ref[slot, hh] = _dot(kj[:, head_rows(hh, HEAD_PAD)], qt)

    def diag_scores(slot, hh, qi):
        lo = qi * t
        qt = qt_ref[head_rows(hh, HEAD_PAD), lo:lo + t]
        s_ref[slot, hh, 0:h2, :] = _dot(k_ref[lo:lo + h2, head_rows(hh, HEAD_PAD)], qt)
        s_ref[slot, hh, h2:t, h2:t] = _dot(k_ref[lo + h2:lo + t, head_rows(hh, HEAD_PAD)], qt[:, h2:])

    def softmax_step(s, m, acc, vj):
        m_new = jnp.maximum(m, jnp.max(s, axis=0, keepdims=True))
        p = jnp.exp2(s - m_new).astype(BF16)
        return m_new, jnp.exp2(m - m_new) * acc + _dot(vj, p)

    def values(hh, j):
        return jnp.concatenate([vt_ref[j, head_rows(hh, V_DIM), :], ones], axis=0)

    def consume(slot, hh, j, m, acc):
        return softmax_step(s_ref[slot, hh], m, acc, values(hh, j))

    def finish(qi, slot, state):
        for hh in range(heads):
            m, acc = state[2 * hh], state[2 * hh + 1]
            vj = values(hh, qi)
            halves = (
                (slice(0, h2), slice(0, h2)),
                (slice(0, t), slice(h2, t)),
            )
            for keys, qcols in halves:
                s = jnp.where(causal[keys, qcols], s_ref[slot, hh, keys, qcols], -jnp.inf)
                _, a = softmax_step(s, m[:, qcols], acc[:, qcols], vj[:, keys])
                o_ref[head_rows(hh, V_DIM), qi * t + qcols.start:qi * t + qcols.stop] = (
                    a[:V_DIM] / a[V_DIM:V_DIM + 1])

    first = 0
    pending = None
    for qi in range(s_len // t):
        cols = slice(qi * t, (qi + 1) * t)
        k0 = k_ref[0:t, :]
        state = []
        for hh in range(heads):
            if qi == 0:
                diag_scores(first, hh, qi)
            else:
                scores(first, hh, k0, cols)
            state += [jnp.full((1, t), -jnp.inf, F32), jnp.zeros((V_DIM + SUM_ROWS, t), F32)]
        if pending is not None:
            finish(*pending)

        def step(j, slot, carry, last=False, cols=cols, qi=qi):
            if not last:
                kn = k_ref[pl.ds(pl.multiple_of((j + 1) * t, t), t), :]
            out = []
            for hh in range(heads):
                if last:
                    diag_scores(1 - slot, hh, qi)
                else:
                    scores(1 - slot, hh, kn, cols)
                out += consume(slot, hh, j, carry[2 * hh], carry[2 * hh + 1])
            return tuple(out)

        def pair(jj, carry, step=step, first=first):
            return step(2 * jj + 1, 1 - first, step(2 * jj, first, carry))

        regular = max(qi - 1, 0)
        state = lax.fori_loop(0, regular // 2, pair, tuple(state))
        if regular % 2:
            state = step(regular - 1, first, state)
        if qi >= 1:
            state = step(qi - 1, (first + qi - 1) % 2, state, last=True)
        diag_slot = (first + qi) % 2
        pending = (qi, diag_slot, state)
        first = 1 - diag_slot
    finish(*pending)


def _mlp_kernel(x_ref, yr_ref, ot_ref, p_ref, gatt_ref, wo1_ref, wo2_ref, gffn_ref, wg_ref,
                wu_ref, wd_ref, gpin_ref, wpg_ref, wpp_ref, gpost_ref, gfin_ref, out_ref,
                acc_ref, vff_ref):
    ya = _rms(ot_ref[...].T, gatt_ref[...]).astype(BF16)
    h = x_ref[...] + _dot(yr_ref[...], wo1_ref[...]) + _dot(ya, wo2_ref[...])
    acc_ref[...] = h
    vff_ref[...] = _rms(h, gffn_ref[...]).astype(BF16)

    def chunk(c, carry):
        cols = pl.ds(pl.multiple_of(c * FF_CHUNK, FF_CHUNK), FF_CHUNK)
        g = _dot(vff_ref[...], wg_ref[:, cols])
        u = _dot(vff_ref[...], wu_ref[:, cols])
        act = (jax.nn.silu(g) * u).astype(BF16)
        acc_ref[...] += _dot(act, wd_ref[cols, :])
        return carry

    lax.fori_loop(0, wg_ref.shape[1] // FF_CHUNK, chunk, 0, unroll=FF_UNROLL)
    h = acc_ref[...]
    e = _rms(_dot(p_ref[...].astype(BF16), wpp_ref[...]), gpost_ref[...])
    gate = jax.nn.sigmoid(_dot(_rms(h, gpin_ref[...]).astype(BF16), wpg_ref[...]))
    out_ref[...] = _rms(h + gate * e, gfin_ref[...])


def _const_spec(shape):
    zeros = (0,) * len(shape)
    return pl.BlockSpec(shape, lambda *_: zeros)


def _tiles(seq):
    proj_t = min(2048, seq)
    att_t = min(512, seq)
    return proj_t, min(RNN_TIME_TILE, seq), att_t, min(512, seq)


def _layer(h, p_l, pos, inv, prm):
    b, s, _ = h.shape
    proj_t, rnn_t, att_t, mlp_t = _tiles(s)
    n = b * s
    cparams = functools.partial(pltpu.CompilerParams, vmem_limit_bytes=VMEM_LIMIT)

    qt, k, vt = pl.pallas_call(
        functools.partial(_proj_kernel, att_t=att_t),
        grid=(b, s // proj_t),
        in_specs=[
            pl.BlockSpec((None, proj_t, D_MODEL), lambda i, j: (i, j, 0)),
            pl.BlockSpec((None, 1, proj_t), lambda i, j: (i, 0, j)),
            _const_spec((QK_ROPE // 2, 1)),
            _const_spec((1, D_MODEL)),
            _const_spec((D_MODEL, D_IN_ATT)),
            _const_spec((1, Q_LORA)),
            _const_spec((MLA_HEADS * HEAD_PAD, Q_LORA)),
            _const_spec((1, KV_LORA)),
            _const_spec((2 * LANES, MLA_HEADS * HEAD_PAD)),
            _const_spec((D_ATT, KV_LORA)),
        ],
        out_specs=[
            pl.BlockSpec((None, MLA_HEADS * HEAD_PAD, proj_t), lambda i, j: (i, 0, j)),
            pl.BlockSpec((None, proj_t, MLA_HEADS * HEAD_PAD), lambda i, j: (i, j, 0)),
            pl.BlockSpec((None, proj_t // att_t, D_ATT, att_t), lambda i, j: (i, j, 0, 0)),
        ],
        out_shape=[
            jax.ShapeDtypeStruct((b, MLA_HEADS * HEAD_PAD, s), BF16),
            jax.ShapeDtypeStruct((b, s, MLA_HEADS * HEAD_PAD), BF16),
            jax.ShapeDtypeStruct((b, s // att_t, D_ATT, att_t), BF16),
        ],
        compiler_params=cparams(dimension_semantics=("arbitrary", "arbitrary")),
        name="proj",
    )(h, pos, inv, prm["g_mix"], prm["w_in_att"], prm["g_q"], prm["wq_t"], prm["g_kv"],
      prm["w_k"], prm["wv_t"])

    rnn_rows = rnn_t * b
    sub_t = min(RNN_SUB_STEPS, rnn_t)
    sub_rows = sub_t * b
    tb = jnp.arange(sub_rows)
    perm = ((tb % b) * sub_t + tb // b)[:, None] == tb[None, :]
    perm = perm.astype(BF16)
    y_rnn = pl.pallas_call(
        _rglru_kernel,
        grid=(s // rnn_t,),
        in_specs=[
            pl.BlockSpec((b, rnn_t, D_MODEL), lambda i: (0, i, 0)),
            _const_spec((1, D_MODEL)),
            _const_spec((D_MODEL, 2 * D_RNN)),
            _const_spec((sub_rows, sub_rows)),
            _const_spec((sub_rows, sub_rows)),
            _const_spec((CONV_W, D_RNN)),
            _const_spec((1, D_RNN)),
            _const_spec((D_RNN // LANES, LANES, 2 * LANES)),
            _const_spec((1, D_RNN)),
            _const_spec((1, D_RNN)),
            _const_spec((1, D_RNN)),
            _const_spec((1, D_RNN)),
        ],
        out_specs=pl.BlockSpec((b, rnn_t, D_RNN), lambda i: (0, i, 0)),
        out_shape=jax.ShapeDtypeStruct((b, s, D_RNN), BF16),
        scratch_shapes=[
            pltpu.VMEM((rnn_rows + (CONV_W - 1) * b, D_RNN), F32),
            pltpu.VMEM((rnn_rows, D_RNN), F32),
            pltpu.VMEM((sub_rows, D_RNN), F32),
            pltpu.VMEM((sub_rows, D_RNN), F32),
            pltpu.VMEM((b, D_RNN), F32),
        ],
        compiler_params=cparams(dimension_semantics=("arbitrary",)),
        name="rglru",
    )(h, prm["g_mix"], prm["w_in_rnn"], perm, perm.T, prm["conv_w"], prm["conv_b"],
      prm["w_gates"], prm["b_a"], prm["b_x"], prm["lru_l"], prm["g_out_rnn"])

    hg = ATT_HEADS_PER_STEP
    o_t = pl.pallas_call(
        functools.partial(_attn_kernel, att_t=att_t, heads=hg),
        grid=(b, MLA_HEADS // hg),
        in_specs=[
            pl.BlockSpec((None, hg * HEAD_PAD, s), lambda i, j: (i, j, 0)),
            pl.BlockSpec((None, s, hg * HEAD_PAD), lambda i, j: (i, 0, j)),
            pl.BlockSpec((None, s // att_t, hg * V_DIM, att_t), lambda i, j: (i, 0, j, 0)),
        ],
        out_specs=pl.BlockSpec((None, hg * V_DIM, s), lambda i, j: (i, j, 0)),
        out_shape=jax.ShapeDtypeStruct((b, D_ATT, s), F32),
        scratch_shapes=[pltpu.VMEM((2, hg, att_t, att_t), F32)],
        compiler_params=cparams(dimension_semantics=("arbitrary", "arbitrary")),
        name="attn",
    )(qt, k, vt)

    spt = s // mlp_t
    out = pl.pallas_call(
        _mlp_kernel,
        grid=(n // mlp_t,),
        in_specs=[
            pl.BlockSpec((mlp_t, D_MODEL), lambda i: (i, 0)),
            pl.BlockSpec((mlp_t, D_RNN), lambda i: (i, 0)),
            pl.BlockSpec((None, D_ATT, mlp_t), lambda i: (i // spt, 0, i % spt)),
            pl.BlockSpec((mlp_t, PLE_DIM), lambda i: (i, 0)),
            _const_spec((1, D_ATT)),
            _const_spec((D_RNN, D_MODEL)),
            _const_spec((D_ATT, D_MODEL)),
            _const_spec((1, D_MODEL)),
            _const_spec((D_MODEL, D_FF)),
            _const_spec((D_MODEL, D_FF)),
            _const_spec((D_FF, D_MODEL)),
            _const_spec((1, D_MODEL)),
            _const_spec((D_MODEL, D_MODEL)),
            _const_spec((PLE_DIM, D_MODEL)),
            _const_spec((1, D_MODEL)),
            _const_spec((1, D_MODEL)),
        ],
        out_specs=pl.BlockSpec((mlp_t, D_MODEL), lambda i: (i, 0)),
        out_shape=jax.ShapeDtypeStruct((n, D_MODEL), F32),
        scratch_shapes=[
            pltpu.VMEM((mlp_t, D_MODEL), F32),
            pltpu.VMEM((mlp_t, D_MODEL), BF16),
        ],
        compiler_params=cparams(dimension_semantics=("arbitrary",)),
        name="mlp",
    )(h.reshape(n, D_MODEL), y_rnn.reshape(n, D_RNN), o_t, p_l.reshape(n, PLE_DIM),
      prm["g_out_att"], prm["w_out_rnn"], prm["w_out_att"], prm["g_ffn"], prm["w_gate"],
      prm["w_up"], prm["w_down"], prm["g_ple_in"], prm["w_ple_gate"], prm["w_ple_proj"], prm["g_ple_post"],
      prm["g_final"])
    return out.reshape(b, s, D_MODEL)


def _prep_layer(l, g_mix, w_in, conv_w, conv_b, w_rg_a, b_rg_a, w_rg_x, b_rg_x, lru_L, g_q_lat,
                w_q_up, g_kv_lat, w_kv_up, g_out_rnn, g_out_att, w_out, g_ffn, w_ffn_gate,
                w_ffn_up, w_ffn_down, g_ple_in, w_ple_gate, w_ple_proj, g_ple_post, g_final):
    row = lambda v: v.reshape(1, -1).astype(F32)
    half = QK_ROPE // 2
    w = w_in[l]
    w_kr = w[:, -QK_ROPE:]
    w_kr_rot = jnp.concatenate([-w_kr[:, half:], w_kr[:, :half]], axis=1)
    w_att = w[:, 2 * D_RNN:]
    w_in_att = jnp.concatenate(
        [w_att, w_kr_rot, jnp.zeros((D_MODEL, D_IN_ATT - w_att.shape[1] - QK_ROPE), F32)], axis=1)

    wq = w_q_up[l].reshape(Q_LORA, MLA_HEADS, QK_DIM)
    wq = jnp.pad(wq, ((0, 0), (0, 0), (0, HEAD_PAD - QK_DIM)))
    wq_t = wq.reshape(Q_LORA, MLA_HEADS * HEAD_PAD).T

    wkv = w_kv_up[l].reshape(KV_LORA, MLA_HEADS, QK_NOPE + V_DIM)
    wk_nope = jnp.pad(wkv[:, :, :QK_NOPE], ((0, 0), (0, 0), (0, HEAD_PAD - QK_NOPE)))
    wk_nope = wk_nope.reshape(KV_LORA, MLA_HEADS * HEAD_PAD)
    place = jnp.pad(jnp.eye(QK_ROPE, dtype=F32), ((0, LANES - QK_ROPE), (QK_NOPE, HEAD_PAD - QK_DIM)))
    w_k = jnp.concatenate([wk_nope, jnp.tile(place, (1, MLA_HEADS))], axis=0)
    wv_t = wkv[:, :, QK_NOPE:].reshape(KV_LORA, D_ATT).T

    def blockdiag(wb):
        wb = wb.reshape(D_RNN // LANES, 2, RNN_BW, RNN_BW)
        z = jnp.zeros_like(wb[:, 0])
        top = jnp.concatenate([wb[:, 0], z], axis=2)
        bot = jnp.concatenate([z, wb[:, 1]], axis=2)
        return jnp.concatenate([top, bot], axis=1)

    w_gates = jnp.concatenate([blockdiag(w_rg_a[l]), blockdiag(w_rg_x[l])], axis=2)

    return dict(
        g_mix=row(g_mix[l]), w_in_att=w_in_att.astype(BF16),
        w_in_rnn=w[:, :2 * D_RNN].astype(BF16), g_q=row(g_q_lat[l]),
        wq_t=wq_t.astype(BF16), g_kv=row(g_kv_lat[l]), w_k=w_k.astype(BF16),
        wv_t=wv_t.astype(BF16),
        conv_w=conv_w[l].astype(F32), conv_b=row(conv_b[l]), w_gates=w_gates.astype(BF16),
        b_a=row(b_rg_a[l]), b_x=row(b_rg_x[l]), lru_l=row(lru_L[l]), g_out_rnn=row(g_out_rnn[l]),
        g_out_att=row(g_out_att[l]), w_out_rnn=w_out[l][:D_RNN].astype(BF16),
        w_out_att=w_out[l][D_RNN:].astype(BF16), g_ffn=row(g_ffn[l]),
        w_gate=w_ffn_gate[l].astype(BF16), w_up=w_ffn_up[l].astype(BF16),
        w_down=w_ffn_down[l].astype(BF16),
        g_ple_in=row(g_ple_in[l]), w_ple_gate=w_ple_gate[l].astype(BF16),
        w_ple_proj=w_ple_proj[l].astype(BF16), g_ple_post=row(g_ple_post[l]),
        g_final=row(g_final),
    )


def kernel(x, p, positions, g_mix, w_in, conv_w, conv_b, w_rg_a, b_rg_a, w_rg_x, b_rg_x, lru_L, g_q_lat, w_q_up, g_kv_lat, w_kv_up, g_out_rnn, g_out_att, w_out, g_ffn, w_ffn_gate, w_ffn_up, w_ffn_down, g_ple_in, w_ple_gate, w_ple_proj, g_ple_post, g_final):
    b, s = positions.shape
    pos = positions.reshape(b, 1, s)
    inv = (ROPE_BASE ** (-jnp.arange(0, QK_ROPE, 2, dtype=F32) / QK_ROPE)).reshape(-1, 1)
    h = x
    depth = p.shape[0]
    assert depth == 1, "the mlp kernel applies the final norm, so only a single layer is supported"
    for l in range(depth):
        prm = _prep_layer(l, g_mix, w_in, conv_w, conv_b, w_rg_a, b_rg_a, w_rg_x, b_rg_x, lru_L,
                          g_q_lat, w_q_up, g_kv_lat, w_kv_up, g_out_rnn, g_out_att, w_out, g_ffn,
                          w_ffn_gate, w_ffn_up, w_ffn_down, g_ple_in, w_ple_gate, w_ple_proj,
                          g_ple_post, g_final)
        h = _layer(h, p[l], pos, inv, prm)
    return h
```

```python
import functools

import jax
import jax.numpy as jnp
from jax import lax
from jax.experimental import pallas as pl
from jax.experimental.pallas import tpu as pltpu

F32 = jnp.float32
BF16 = jnp.bfloat16

EPS = 1e-6
D_MODEL = 1024
D_RNN = 512
RNN_BLOCKS = 8
RNN_BW = 64
CONV_W = 4
LRU_C = 8.0
MLA_HEADS = 8
QK_NOPE = 64
QK_ROPE = 32
V_DIM = 64
Q_LORA = 256
KV_LORA = 128
D_ATT = MLA_HEADS * V_DIM
QK_DIM = QK_NOPE + QK_ROPE
ATT_SCALE = QK_DIM ** -0.5
ROPE_BASE = 10000.0
D_FF = 2816
PLE_DIM = 256

LANES = 128
SUBLANES = 8
HEAD_PAD = LANES
D_IN_ATT = 512
PROJ_SUB_TILE = 512
RNN_TIME_TILE = 128
RNN_SUB_STEPS = 16
FF_CHUNK = 256
FF_UNROLL = 5
VMEM_LIMIT = 56 * 1024 * 1024
ATT_HEADS_PER_STEP = 4
SUM_ROWS = 16
LOG2E = 1.4426950408889634
Q_SCALE = ATT_SCALE * LOG2E


def _rms(x, g):
    ms = jnp.mean(x * x, axis=-1, keepdims=True)
    return x * lax.rsqrt(ms + EPS) * g


def _dot(a, b):
    return jnp.dot(a, b, preferred_element_type=F32)


def _dot_nt(a, b):
    return lax.dot_general(a, b, (((1,), (1,)), ((), ())), preferred_element_type=F32)


def _proj_kernel(x_ref, pos_ref, inv_ref, gmix_ref, win_ref, gq_ref, wqt_ref, gkv_ref,
                 wk_ref, wvt_ref, qt_ref, k_ref, vt_ref, *, att_t):
    half = QK_ROPE // 2
    sub = min(PROJ_SUB_TILE, att_t)
    for i in range(x_ref.shape[0] // sub):
        rows = slice(i * sub, (i + 1) * sub)
        ang = inv_ref[...] * pos_ref[:, rows].astype(F32)
        sin_t, cos_t = jnp.sin(ang), jnp.cos(ang)
        u = _rms(x_ref[rows, :], gmix_ref[...]).astype(BF16)
        z = _dot(u, win_ref[...])
        c_q = z[:, :Q_LORA]
        c_kv = z[:, Q_LORA:Q_LORA + KV_LORA]
        zk = z[:, Q_LORA + KV_LORA:]

        cqn = _rms(c_q, gq_ref[...]).astype(BF16)
        qt = _dot_nt(wqt_ref[...], cqn)
        for h in range(MLA_HEADS):
            base = h * HEAD_PAD
            r0 = base + QK_NOPE
            qt_ref[base:r0, rows] = (qt[base:r0] * Q_SCALE).astype(BF16)
            t1 = qt[r0:r0 + half]
            t2 = qt[r0 + half:r0 + QK_ROPE]
            qt_ref[r0:r0 + half, rows] = ((t1 * cos_t - t2 * sin_t) * Q_SCALE).astype(BF16)
            qt_ref[r0 + half:r0 + QK_ROPE, rows] = ((t1 * sin_t + t2 * cos_t) * Q_SCALE).astype(BF16)
            qt_ref[r0 + QK_ROPE:base + HEAD_PAD, rows] = jnp.zeros((HEAD_PAD - QK_DIM, sub), BF16)

        ckvn = _rms(c_kv, gkv_ref[...]).astype(BF16)
        tabk = jnp.concatenate(
            [cos_t, cos_t, sin_t, sin_t, jnp.zeros((LANES - 2 * QK_ROPE, sub), F32)], axis=0).T
        prod = zk * tabk
        kr = prod + pltpu.roll(prod, LANES - QK_ROPE, 1)
        kcat = jnp.concatenate([ckvn, kr.astype(BF16)], axis=1)
        k_ref[rows, :] = _dot(kcat, wk_ref[...]).astype(BF16)
        vt = _dot_nt(wvt_ref[...], ckvn).astype(BF16)
        start = i * sub
        vt_ref[start // att_t, :, start % att_t:start % att_t + sub] = vt


def _sigmoid(x):
    return 0.5 * jnp.tanh(0.5 * x) + 0.5


def _gelu_tanh(x):
    c0 = (2.0 / jnp.pi) ** 0.5
    hx = 0.5 * x
    return hx + hx * jnp.tanh(x * (c0 + (c0 * 0.044715) * (x * x)))


def _rglru_kernel(x_ref, gmix_ref, win_ref, perm_ref, permt_ref, convw_ref, convb_ref, wg_ref,
                  ba_ref, bx_ref, lru_ref, gout_ref, y_ref, xpad_ref, gate_ref, a_ref, b_ref, h_ref):
    nb, tt, _ = x_ref.shape
    ts = min(RNN_SUB_STEPS, tt)
    sub = nb * ts
    halo = (CONV_W - 1) * nb

    @pl.when(pl.program_id(0) == 0)
    def _():
        xpad_ref[0:halo, :] = jnp.zeros((halo, D_RNN), F32)
        h_ref[...] = jnp.zeros_like(h_ref)

    neg_l = -lru_ref[...]
    sp = jnp.maximum(neg_l, 0.0) + jnp.log1p(jnp.exp(-jnp.abs(neg_l)))
    w = convw_ref[...]

    def project(q):
        xq = x_ref[:, q * ts:(q + 1) * ts, :].reshape(sub, D_MODEL)
        u = _rms(xq, gmix_ref[...]).astype(BF16)
        zr = _dot(perm_ref[...], _dot(u, win_ref[...]).astype(BF16))
        xpad_ref[halo + q * sub:halo + (q + 1) * sub, :] = zr[:, :D_RNN]
        gate_ref[q * sub:(q + 1) * sub, :] = zr[:, D_RNN:]

    def recur(q, h):
        lo = halo + q * sub
        xc = convb_ref[...] + w[CONV_W - 1:CONV_W] * xpad_ref[lo:lo + sub, :]
        for j in range(1, CONV_W):
            xc = xc + w[CONV_W - 1 - j:CONV_W - j] * xpad_ref[lo - j * nb:lo - j * nb + sub, :]
        xcb = xc.astype(BF16)
        for g in range(D_RNN // LANES):
            sl = slice(g * LANES, (g + 1) * LANES)
            pre = _dot(xcb[:, sl], wg_ref[g])
            r = _sigmoid(pre[:, :LANES] + ba_ref[:, sl])
            i = _sigmoid(pre[:, LANES:] + bx_ref[:, sl])
            neg_log_a = (LRU_C * r) * sp[:, sl]
            a = jnp.exp2(neg_log_a * -LOG2E)
            a_ref[:, sl] = a
            qq = jnp.tanh(neg_log_a) * (1.0 + a * a)
            mult = jnp.where(qq > 0.0, qq * lax.rsqrt(qq), 0.0)
            b_ref[:, sl] = mult * (i * xc[:, sl])
        for ti in range(ts):
            at = slice(ti * nb, (ti + 1) * nb)
            h = a_ref[at, :] * h + b_ref[at, :]
            b_ref[at, :] = h
        y = _rms(b_ref[...] * _gelu_tanh(gate_ref[q * sub:(q + 1) * sub, :]), gout_ref[...])
        y = _dot(permt_ref[...], y.astype(BF16)).astype(BF16)
        y_ref[:, q * ts:(q + 1) * ts, :] = y.reshape(nb, ts, D_RNN)
        return h

    n_sub = tt // ts
    h = h_ref[...]
    project(0)
    for q in range(n_sub):
        if q + 1 < n_sub:
            project(q + 1)
        h = recur(q, h)
    h_ref[...] = h
    xpad_ref[0:halo, :] = xpad_ref[nb * tt:, :]


def _attn_kernel(qt_ref, k_ref, vt_ref, o_ref, s_ref, *, att_t, heads):
    s_len = k_ref.shape[0]
    t = att_t
    h2 = t // 2
    kidx = lax.broadcasted_iota(jnp.int32, (t, t), 0)
    qidx = lax.broadcasted_iota(jnp.int32, (t, t), 1)
    causal = kidx <= qidx
    ones = jnp.ones((SUM_ROWS, t), BF16)

    def head_rows(hh, width):
        return slice(hh * width, (hh + 1) * width)

    def scores(slot, hh, kj, cols):
        qt = qt_ref[head_rows(hh, HEAD_PAD), cols]
        s_ref[slot, hh] = _dot(kj[:, head_rows(hh, HEAD_PAD)], qt)

    def diag_scores(slot, hh, qi):
        lo = qi * t
        qt = qt_ref[head_rows(hh, HEAD_PAD), lo:lo + t]
        s_ref[slot, hh, 0:h2, :] = _dot(k_ref[lo:lo + h2, head_rows(hh, HEAD_PAD)], qt)
        s_ref[slot, hh, h2:t, h2:t] = _dot(k_ref[lo + h2:lo + t, head_rows(hh, HEAD_PAD)], qt[:, h2:])

    def softmax_step(s, m, acc, vj):
        m_new = jnp.maximum(m, jnp.max(s, axis=0, keepdims=True))
        p = jnp.exp2(s - m_new).astype(BF16)
        return m_new, jnp.exp2(m - m_new) * acc + _dot(vj, p)

    def values(hh, j):
        return jnp.concatenate([vt_ref[j, head_rows(hh, V_DIM), :], ones], axis=0)

    def consume(slot, hh, j, m, acc):
        return softmax_step(s_ref[slot, hh], m, acc, values(hh, j))

    def finish(qi, slot, state):
        for hh in range(heads):
            m, acc = state[2 * hh], state[2 * hh + 1]
            vj = values(hh, qi)
            halves = (
                (slice(0, h2), slice(0, h2)),
                (slice(0, t), slice(h2, t)),
            )
            for keys, qcols in halves:
                s = jnp.where(causal[keys, qcols], s_ref[slot, hh, keys, qcols], -jnp.inf)
                _, a = softmax_step(s, m[:, qcols], acc[:, qcols], vj[:, keys])
                o_ref[head_rows(hh, V_DIM), qi * t + qcols.start:qi * t + qcols.stop] = (
                    a[:V_DIM] / a[V_DIM:V_DIM + 1])

    first = 0
    pending = None
    for qi in range(s_len // t):
        cols = slice(qi * t, (qi + 1) * t)
        k0 = k_ref[0:t, :]
        state = []
        for hh in range(heads):
            if qi == 0:
                diag_scores(first, hh, qi)
            else:
                scores(first, hh, k0, cols)
            state += [jnp.full((1, t), -jnp.inf, F32), jnp.zeros((V_DIM + SUM_ROWS, t), F32)]
        if pending is not None:
            finish(*pending)

        def step(j, slot, carry, last=False, cols=cols, qi=qi):
            if not last:
                kn = k_ref[pl.ds(pl.multiple_of((j + 1) * t, t), t), :]
            out = []
            for hh in range(heads):
                if last:
                    diag_scores(1 - slot, hh, qi)
                else:
                    scores(1 - slot, hh, kn, cols)
                out += consume(slot, hh, j, carry[2 * hh], carry[2 * hh + 1])
            return tuple(out)

        def pair(jj, carry, step=step, first=first):
            return step(2 * jj + 1, 1 - first, step(2 * jj, first, carry))

        regular = max(qi - 1, 0)
        state = lax.fori_loop(0, regular // 2, pair, tuple(state))
        if regular % 2:
            state = step(regular - 1, first, state)
        if qi >= 1:
            state = step(qi - 1, (first + qi - 1) % 2, state, last=True)
        diag_slot = (first + qi) % 2
        pending = (qi, diag_slot, state)
        first = 1 - diag_slot
    finish(*pending)


def _mlp_kernel(x_ref, yr_ref, ot_ref, p_ref, gatt_ref, wo1_ref, wo2_ref, gffn_ref, wg_ref,
                wu_ref, wd_ref, gpin_ref, wpg_ref, wpp_ref, gpost_ref, gfin_ref, out_ref,
                acc_ref, vff_ref):
    ya = _rms(ot_ref[...].T, gatt_ref[...]).astype(BF16)
    h = x_ref[...] + _dot(yr_ref[...], wo1_ref[...]) + _dot(ya, wo2_ref[...])
    acc_ref[...] = h
    vff_ref[...] = _rms(h, gffn_ref[...]).astype(BF16)

    def chunk(c, carry):
        cols = pl.ds(pl.multiple_of(c * FF_CHUNK, FF_CHUNK), FF_CHUNK)
        g = _dot(vff_ref[...], wg_ref[:, cols])
        u = _dot(vff_ref[...], wu_ref[:, cols])
        act = (jax.nn.silu(g) * u).astype(BF16)
        acc_ref[...] += _dot(act, wd_ref[cols, :])
        return carry

    lax.fori_loop(0, wg_ref.shape[1] // FF_CHUNK, chunk, 0, unroll=FF_UNROLL)
    h = acc_ref[...]
    e = _rms(_dot(p_ref[...].astype(BF16), wpp_ref[...]), gpost_ref[...])
    gate = jax.nn.sigmoid(_dot(_rms(h, gpin_ref[...]).astype(BF16), wpg_ref[...]))
    out_ref[...] = _rms(h + gate * e, gfin_ref[...])


def _const_spec(shape):
    zeros = (0,) * len(shape)
    return pl.BlockSpec(shape, lambda *_: zeros)


def _tiles(seq):
    proj_t = min(2048, seq)
    att_t = min(512, seq)
    return proj_t, min(RNN_TIME_TILE, seq), att_t, min(512, seq)


def _layer(h, p_l, pos, inv, prm):
    b, s, _ = h.shape
    proj_t, rnn_t, att_t, mlp_t = _tiles(s)
    n = b * s
    cparams = functools.partial(pltpu.CompilerParams, vmem_limit_bytes=VMEM_LIMIT)

    qt, k, vt = pl.pallas_call(
        functools.partial(_proj_kernel, att_t=att_t),
        grid=(b, s // proj_t),
        in_specs=[
            pl.BlockSpec((None, proj_t, D_MODEL), lambda i, j: (i, j, 0)),
            pl.BlockSpec((None, 1, proj_t), lambda i, j: (i, 0, j)),
            _const_spec((QK_ROPE // 2, 1)),
            _const_spec((1, D_MODEL)),
            _const_spec((D_MODEL, D_IN_ATT)),
            _const_spec((1, Q_LORA)),
            _const_spec((MLA_HEADS * HEAD_PAD, Q_LORA)),
            _const_spec((1, KV_LORA)),
            _const_spec((2 * LANES, MLA_HEADS * HEAD_PAD)),
            _const_spec((D_ATT, KV_LORA)),
        ],
        out_specs=[
            pl.BlockSpec((None, MLA_HEADS * HEAD_PAD, proj_t), lambda i, j: (i, 0, j)),
            pl.BlockSpec((None, proj_t, MLA_HEADS * HEAD_PAD), lambda i, j: (i, j, 0)),
            pl.BlockSpec((None, proj_t // att_t, D_ATT, att_t), lambda i, j: (i, j, 0, 0)),
        ],
        out_shape=[
            jax.ShapeDtypeStruct((b, MLA_HEADS * HEAD_PAD, s), BF16),
            jax.ShapeDtypeStruct((b, s, MLA_HEADS * HEAD_PAD), BF16),
            jax.ShapeDtypeStruct((b, s // att_t, D_ATT, att_t), BF16),
        ],
        compiler_params=cparams(dimension_semantics=("arbitrary", "arbitrary")),
        name="proj",
    )(h, pos, inv, prm["g_mix"], prm["w_in_att"], prm["g_q"], prm["wq_t"], prm["g_kv"],
      prm["w_k"], prm["wv_t"])

    rnn_rows = rnn_t * b
    sub_t = min(RNN_SUB_STEPS, rnn_t)
    sub_rows = sub_t * b
    tb = jnp.arange(sub_rows)
    perm = ((tb % b) * sub_t + tb // b)[:, None] == tb[None, :]
    perm = perm.astype(BF16)
    y_rnn = pl.pallas_call(
        _rglru_kernel,
        grid=(s // rnn_t,),
        in_specs=[
            pl.BlockSpec((b, rnn_t, D_MODEL), lambda i: (0, i, 0)),
            _const_spec((1, D_MODEL)),
            _const_spec((D_MODEL, 2 * D_RNN)),
            _const_spec((sub_rows, sub_rows)),
            _const_spec((sub_rows, sub_rows)),
            _const_spec((CONV_W, D_RNN)),
            _const_spec((1, D_RNN)),
            _const_spec((D_RNN // LANES, LANES, 2 * LANES)),
            _const_spec((1, D_RNN)),
            _const_spec((1, D_RNN)),
            _const_spec((1, D_RNN)),
            _const_spec((1, D_RNN)),
        ],
        out_specs=pl.BlockSpec((b, rnn_t, D_RNN), lambda i: (0, i, 0)),
        out_shape=jax.ShapeDtypeStruct((b, s, D_RNN), BF16),
        scratch_shapes=[
            pltpu.VMEM((rnn_rows + (CONV_W - 1) * b, D_RNN), F32),
            pltpu.VMEM((rnn_rows, D_RNN), F32),
            pltpu.VMEM((sub_rows, D_RNN), F32),
            pltpu.VMEM((sub_rows, D_RNN), F32),
            pltpu.VMEM((b, D_RNN), F32),
        ],
        compiler_params=cparams(dimension_semantics=("arbitrary",)),
        name="rglru",
    )(h, prm["g_mix"], prm["w_in_rnn"], perm, perm.T, prm["conv_w"], prm["conv_b"],
      prm["w_gates"], prm["b_a"], prm["b_x"], prm["lru_l"], prm["g_out_rnn"])

    hg = ATT_HEADS_PER_STEP
    o_t = pl.pallas_call(
        functools.partial(_attn_kernel, att_t=att_t, heads=hg),
        grid=(b, MLA_HEADS // hg),
        in_specs=[
            pl.BlockSpec((None, hg * HEAD_PAD, s), lambda i, j: (i, j, 0)),
            pl.BlockSpec((None, s, hg * HEAD_PAD), lambda i, j: (i, 0, j)),
            pl.BlockSpec((None, s // att_t, hg * V_DIM, att_t), lambda i, j: (i, 0, j, 0)),
        ],
        out_specs=pl.BlockSpec((None, hg * V_DIM, s), lambda i, j: (i, j, 0)),
        out_shape=jax.ShapeDtypeStruct((b, D_ATT, s), F32),
        scratch_shapes=[pltpu.VMEM((2, hg, att_t, att_t), F32)],
        compiler_params=cparams(dimension_semantics=("arbitrary", "arbitrary")),
        name="attn",
    )(qt, k, vt)

    spt = s // mlp_t
    out = pl.pallas_call(
        _mlp_kernel,
        grid=(n // mlp_t,),
        in_specs=[
            pl.BlockSpec((mlp_t, D_MODEL), lambda i: (i, 0)),
            pl.BlockSpec((mlp_t, D_RNN), lambda i: (i, 0)),
            pl.BlockSpec((None, D_ATT, mlp_t), lambda i: (i // spt, 0, i % spt)),
            pl.BlockSpec((mlp_t, PLE_DIM), lambda i: (i, 0)),
            _const_spec((1, D_ATT)),
            _const_spec((D_RNN, D_MODEL)),
            _const_spec((D_ATT, D_MODEL)),
            _const_spec((1, D_MODEL)),
            _const_spec((D_MODEL, D_FF)),
            _const_spec((D_MODEL, D_FF)),
            _const_spec((D_FF, D_MODEL)),
            _const_spec((1, D_MODEL)),
            _const_spec((D_MODEL, D_MODEL)),
            _const_spec((PLE_DIM, D_MODEL)),
            _const_spec((1, D_MODEL)),
            _const_spec((1, D_MODEL)),
        ],
        out_specs=pl.BlockSpec((mlp_t, D_MODEL), lambda i: (i, 0)),
        out_shape=jax.ShapeDtypeStruct((n, D_MODEL), F32),
        scratch_shapes=[
            pltpu.VMEM((mlp_t, D_MODEL), F32),
            pltpu.VMEM((mlp_t, D_MODEL), BF16),
        ],
        compiler_params=cparams(dimension_semantics=("arbitrary",)),
        name="mlp",
    )(h.reshape(n, D_MODEL), y_rnn.reshape(n, D_RNN), o_t, p_l.reshape(n, PLE_DIM),
      prm["g_out_att"], prm["w_out_rnn"], prm["w_out_att"], prm["g_ffn"], prm["w_gate"],
      prm["w_up"], prm["w_down"], prm["g_ple_in"], prm["w_ple_gate"], prm["w_ple_proj"], prm["g_ple_post"],
      prm["g_final"])
    return out.reshape(b, s, D_MODEL)


def _prep_layer(l, g_mix, w_in, conv_w, conv_b, w_rg_a, b_rg_a, w_rg_x, b_rg_x, lru_L, g_q_lat,
                w_q_up, g_kv_lat, w_kv_up, g_out_rnn, g_out_att, w_out, g_ffn, w_ffn_gate,
                w_ffn_up, w_ffn_down, g_ple_in, w_ple_gate, w_ple_proj, g_ple_post, g_final):
    row = lambda v: v.reshape(1, -1).astype(F32)
    half = QK_ROPE // 2
    w = w_in[l]
    w_kr = w[:, -QK_ROPE:]
    w_kr_rot = jnp.concatenate([-w_kr[:, half:], w_kr[:, :half]], axis=1)
    w_att = w[:, 2 * D_RNN:]
    w_in_att = jnp.concatenate(
        [w_att, w_kr_rot, jnp.zeros((D_MODEL, D_IN_ATT - w_att.shape[1] - QK_ROPE), F32)], axis=1)

    wq = w_q_up[l].reshape(Q_LORA, MLA_HEADS, QK_DIM)
    wq = jnp.pad(wq, ((0, 0), (0, 0), (0, HEAD_PAD - QK_DIM)))
    wq_t = wq.reshape(Q_LORA, MLA_HEADS * HEAD_PAD).T

    wkv = w_kv_up[l].reshape(KV_LORA, MLA_HEADS, QK_NOPE + V_DIM)
    wk_nope = jnp.pad(wkv[:, :, :QK_NOPE], ((0, 0), (0, 0), (0, HEAD_PAD - QK_NOPE)))
    wk_nope = wk_nope.reshape(KV_LORA, MLA_HEADS * HEAD_PAD)
    place = jnp.pad(jnp.eye(QK_ROPE, dtype=F32), ((0, LANES - QK_ROPE), (QK_NOPE, HEAD_PAD - QK_DIM)))
    w_k = jnp.concatenate([wk_nope, jnp.tile(place, (1, MLA_HEADS))], axis=0)
    wv_t = wkv[:, :, QK_NOPE:].reshape(KV_LORA, D_ATT).T

    def blockdiag(wb):
        wb = wb.reshape(D_RNN // LANES, 2, RNN_BW, RNN_BW)
        z = jnp.zeros_like(wb[:, 0])
        top = jnp.concatenate([wb[:, 0], z], axis=2)
        bot = jnp.concatenate([z, wb[:, 1]], axis=2)
        return jnp.concatenate([top, bot], axis=1)

    w_gates = jnp.concatenate([blockdiag(w_rg_a[l]), blockdiag(w_rg_x[l])], axis=2)

    return dict(
        g_mix=row(g_mix[l]), w_in_att=w_in_att.astype(BF16),
        w_in_rnn=w[:, :2 * D_RNN].astype(BF16), g_q=row(g_q_lat[l]),
        wq_t=wq_t.astype(BF16), g_kv=row(g_kv_lat[l]), w_k=w_k.astype(BF16),
        wv_t=wv_t.astype(BF16),
        conv_w=conv_w[l].astype(F32), conv_b=row(conv_b[l]), w_gates=w_gates.astype(BF16),
        b_a=row(b_rg_a[l]), b_x=row(b_rg_x[l]), lru_l=row(lru_L[l]), g_out_rnn=row(g_out_rnn[l]),
        g_out_att=row(g_out_att[l]), w_out_rnn=w_out[l][:D_RNN].astype(BF16),
        w_out_att=w_out[l][D_RNN:].astype(BF16), g_ffn=row(g_ffn[l]),
        w_gate=w_ffn_gate[l].astype(BF16), w_up=w_ffn_up[l].astype(BF16),
        w_down=w_ffn_down[l].astype(BF16),
        g_ple_in=row(g_ple_in[l]), w_ple_gate=w_ple_gate[l].astype(BF16),
        w_ple_proj=w_ple_proj[l].astype(BF16), g_ple_post=row(g_ple_post[l]),
        g_final=row(g_final),
    )


def kernel(x, p, positions, g_mix, w_in, conv_w, conv_b, w_rg_a, b_rg_a, w_rg_x, b_rg_x, lru_L, g_q_lat, w_q_up, g_kv_lat, w_kv_up, g_out_rnn, g_out_att, w_out, g_ffn, w_ffn_gate, w_ffn_up, w_ffn_down, g_ple_in, w_ple_gate, w_ple_proj, g_ple_post, g_final):
    b, s = positions.shape
    pos = positions.reshape(b, 1, s)
    inv = (ROPE_BASE ** (-jnp.arange(0, QK_ROPE, 2, dtype=F32) / QK_ROPE)).reshape(-1, 1)
    h = x
    depth = p.shape[0]
    assert depth == 1, "the mlp kernel applies the final norm, so only a single layer is supported"
    for l in range(depth):
        prm = _prep_layer(l, g_mix, w_in, conv_w, conv_b, w_rg_a, b_rg_a, w_rg_x, b_rg_x, lru_L,
                          g_q_lat, w_q_up, g_kv_lat, w_kv_up, g_out_rnn, g_out_att, w_out, g_ffn,
                          w_ffn_gate, w_ffn_up, w_ffn_down, g_ple_in, w_ple_gate, w_ple_proj,
                          g_ple_post, g_final)
        h = _layer(h, p[l], pos, inv, prm)
    return h
```
